```python
import jax, jax.numpy as jnp
from jax import lax
import numpy as np

D_MODEL = 1024
BATCH = 8
SEQ = 2048
DEPTH = 1
DEC_BATCH = 128
DEC_SEQ = 1
PAST_LEN = 16384
PAGE_SIZE = 128

POOL_WINDOWS = (2, 4, 8, 16)
POOL_GROUPS = len(POOL_WINDOWS)
D_POOL = D_MODEL
POOL_GW = D_POOL // POOL_GROUPS
POOL_BUF = max(POOL_WINDOWS) - 1
SSD_EXPAND = 2
D_INNER = SSD_EXPAND * D_MODEL
SSD_HEADDIM = 64
SSD_HEADS = D_INNER // SSD_HEADDIM
SSD_GROUPS = 8
SSD_HPG = SSD_HEADS // SSD_GROUPS
D_STATE = 128
CONV_W = 4
CONV_DIM = D_INNER + 2 * SSD_GROUPS * D_STATE
SSD_CHUNK = 128
D_FF = ((8 * D_MODEL // 3 + 255) // 256) * 256
N_BRANCH = 2
N_MOD = 6
COL_Z = D_POOL
COL_XBC = COL_Z + D_INNER
COL_DT = COL_XBC + CONV_DIM
COL_GATE = COL_DT + SSD_HEADS
IN_COLS = COL_GATE + N_BRANCH * D_MODEL
EPS = 1e-6

kernel_name = "hybrid_pool_ssd_adaln_decoder_step"


def rmsnorm(x, w):
    xf = x.astype(jnp.float32)
    y = xf * lax.rsqrt(jnp.mean(xf * xf, axis=-1, keepdims=True) + EPS) * w.astype(jnp.float32)
    return y.astype(x.dtype)


def pool_mixer(u, prev, pos0, w_pool, pool_scale):
    bsz, L, _ = u.shape
    full = jnp.concatenate([prev.astype(u.dtype), u], axis=1)
    cs = jnp.cumsum(full.astype(jnp.float32), axis=1)
    cs = jnp.pad(cs, ((0, 0), (1, 0), (0, 0)))
    pos = pos0 + jnp.arange(L)
    hi = cs[:, POOL_BUF + 1:POOL_BUF + 1 + L]
    outs = []
    for g, w in enumerate(POOL_WINDOWS):
        sl = slice(g * POOL_GW, (g + 1) * POOL_GW)
        lo = cs[:, POOL_BUF + 1 - w:POOL_BUF + 1 - w + L, sl]
        cnt = jnp.minimum(pos + 1, w).astype(jnp.float32)[None, :, None]
        outs.append((hi[:, :, sl] - lo) / cnt)
    pooled = jnp.concatenate(outs, axis=-1) - u.astype(jnp.float32)
    pg = pooled.reshape(bsz, L, POOL_GROUPS, POOL_GW)
    y = jnp.einsum("blgi,gio->blgo", pg, w_pool.astype(jnp.float32)).reshape(bsz, L, D_MODEL)
    y = y * pool_scale.astype(jnp.float32)
    return y.astype(u.dtype), full[:, -POOL_BUF:]


def causal_conv(xbc, prev, w, b):
    L = xbc.shape[1]
    full = jnp.concatenate([prev.astype(xbc.dtype), xbc], axis=1)
    out = b
    for k in range(CONV_W):
        out = out + full[:, k:k + L] * w[k]
    return jax.nn.silu(out), full[:, -(CONV_W - 1):]


def ssd_scan(xh, dt, A, Bm, Cm, h0):
    bsz, L = xh.shape[:2]
    Q = min(SSD_CHUNK, L)
    pad = (-L) % Q
    xh, dt, Bm, Cm = (t.astype(jnp.float32) for t in (xh, dt, Bm, Cm))
    if pad:
        padw = lambda t: jnp.pad(t, [(0, 0), (0, pad)] + [(0, 0)] * (t.ndim - 2))
        xh, dt, Bm, Cm = padw(xh), padw(dt), padw(Bm), padw(Cm)
    nc = (L + pad) // Q
    x = xh.reshape(bsz, nc, Q, SSD_GROUPS, SSD_HPG, SSD_HEADDIM)
    dtc = dt.reshape(bsz, nc, Q, SSD_GROUPS, SSD_HPG)
    Bc = Bm.reshape(bsz, nc, Q, SSD_GROUPS, D_STATE)
    Cc = Cm.reshape(bsz, nc, Q, SSD_GROUPS, D_STATE)
    dA = dtc * A.astype(jnp.float32).reshape(SSD_GROUPS, SSD_HPG)
    Acs = jnp.cumsum(dA, axis=2)
    xdt = x * dtc[..., None]
    mask = jnp.tril(jnp.ones((Q, Q), dtype=bool))[None, None, :, :, None, None]
    seg = Acs[:, :, :, None] - Acs[:, :, None]
    decay = jnp.exp(jnp.where(mask, seg, -jnp.inf))
    CB = jnp.einsum("bclgn,bcsgn->bclsg", Cc, Bc)
    scores = CB[..., None] * decay
    y_diag = jnp.einsum("bclsgr,bcsgrp->bclgrp", scores, xdt)
    decay_to_end = jnp.exp(Acs[:, :, -1:] - Acs)
    chunk_states = jnp.einsum("bclgn,bclgr,bclgrp->bcgrpn", Bc, decay_to_end, xdt)
    chunk_decay = jnp.exp(Acs[:, :, -1])
    h0r = h0.astype(jnp.float32).reshape(bsz, SSD_GROUPS, SSD_HPG, SSD_HEADDIM, D_STATE)

    def step(h, inp):
        st, dec = inp
        return h * dec[..., None, None] + st, h

    h_last, h_prev = lax.scan(step, h0r, (jnp.moveaxis(chunk_states, 1, 0), jnp.moveaxis(chunk_decay, 1, 0)))
    h_prev = jnp.moveaxis(h_prev, 0, 1)
    y_off = jnp.einsum("bclgn,bcgrpn,bclgr->bclgrp", Cc, h_prev, jnp.exp(Acs))
    y = (y_diag + y_off).reshape(bsz, nc * Q, SSD_HEADS, SSD_HEADDIM)[:, :L]
    return y, h_last.reshape(bsz, SSD_HEADS, SSD_HEADDIM, D_STATE)


def decoder_layer(x, c, pool_prev, conv_prev, ssm_prev, pos0, w_ada, b_ada, norm1_w, w_in, w_pool, pool_scale,
                  conv_w, conv_b, dt_bias, A_log, D_skip, ssd_norm_w, w_ssd_proj, w_out, norm2_w, w_ffn_in, w_ffn_out):
    bsz, L, _ = x.shape
    mod = jax.nn.silu(c) @ w_ada + b_ada
    sh1, sc1, g1, sh2, sc2, g2 = (m[:, None, :] for m in jnp.split(mod, N_MOD, axis=-1))
    h = rmsnorm(x, norm1_w) * (1 + sc1) + sh1
    proj = h @ w_in
    u, z, xbc, dt_raw, gates = jnp.split(proj, [COL_Z, COL_XBC, COL_DT, COL_GATE], axis=-1)
    a_out, pool_new = pool_mixer(u, pool_prev, pos0, w_pool, pool_scale)
    xbc, conv_new = causal_conv(xbc, conv_prev, conv_w, conv_b)
    xs, Bm, Cm = jnp.split(xbc, [D_INNER, D_INNER + SSD_GROUPS * D_STATE], axis=-1)
    dt = jax.nn.softplus(dt_raw.astype(jnp.float32) + dt_bias.astype(jnp.float32))
    A = -jnp.exp(A_log.astype(jnp.float32))
    xh = xs.reshape(bsz, L, SSD_HEADS, SSD_HEADDIM)
    y, ssm_new = ssd_scan(xh, dt, A, Bm.reshape(bsz, L, SSD_GROUPS, D_STATE),
                          Cm.reshape(bsz, L, SSD_GROUPS, D_STATE), ssm_prev)
    y = y + D_skip.astype(jnp.float32)[:, None] * xh.astype(jnp.float32)
    y = y.reshape(bsz, L, D_INNER).astype(x.dtype)
    y = rmsnorm(y * jax.nn.silu(z), ssd_norm_w)
    b_out = y @ w_ssd_proj
    gate_a, gate_b = jnp.split(jax.nn.sigmoid(gates), N_BRANCH, axis=-1)
    mix = (gate_a * a_out + gate_b * b_out) @ w_out
    x = x + g1 * mix
    h2 = rmsnorm(x, norm2_w) * (1 + sc2) + sh2
    gt, up = jnp.split(h2 @ w_ffn_in, 2, axis=-1)
    x = x + g2 * ((jax.nn.silu(gt) * up) @ w_ffn_out)
    return x, pool_new, conv_new, ssm_new.astype(ssm_prev.dtype)


def setup_inputs(seed: int = 0) -> dict:
    key = jax.random.key(seed)
    ks = iter(jax.random.split(key, 32))

    def nrm(shape, scale):
        return scale * jax.random.normal(next(ks), shape, jnp.float32)

    def unif(shape, lo, hi):
        return jax.random.uniform(next(ks), shape, jnp.float32, lo, hi)

    dt0 = jnp.exp(unif((DEPTH, SSD_HEADS), float(np.log(1e-3)), float(np.log(1e-1))))
    return {
        "x_prompt": nrm((BATCH, SEQ, D_MODEL), 1.0),
        "x_sample": nrm((DEC_BATCH, DEC_SEQ, D_MODEL), 1.0),
        "c_prompt": nrm((BATCH, D_MODEL), 1.0),
        "c_sample": nrm((DEC_BATCH, D_MODEL), 1.0),
        "state_pool": nrm((DEPTH, DEC_BATCH, POOL_BUF, D_POOL), 1.0),
        "state_conv": nrm((DEPTH, DEC_BATCH, CONV_W - 1, CONV_DIM), 1.0),
        "state_ssm": nrm((DEPTH, DEC_BATCH, SSD_HEADS, SSD_HEADDIM, D_STATE), 0.3),
        "w_ada": nrm((DEPTH, D_MODEL, N_MOD * D_MODEL), 0.5 * D_MODEL ** -0.5),
        "b_ada": nrm((DEPTH, N_MOD * D_MODEL), 0.01),
        "norm1_w": 1.0 + nrm((DEPTH, D_MODEL), 0.05),
        "w_in": nrm((DEPTH, D_MODEL, IN_COLS), D_MODEL ** -0.5),
        "w_pool": nrm((DEPTH, POOL_GROUPS, POOL_GW, POOL_GW), POOL_GW ** -0.5),
        "pool_scale": 0.5 + nrm((DEPTH, D_MODEL), 0.05),
        "conv_w": nrm((DEPTH, CONV_W, CONV_DIM), CONV_W ** -0.5),
        "conv_b": nrm((DEPTH, CONV_DIM), 0.01),
        "dt_bias": dt0 + jnp.log(-jnp.expm1(-dt0)),
        "A_log": jnp.log(unif((DEPTH, SSD_HEADS), 1.0, 16.0)),
        "D_skip": 1.0 + nrm((DEPTH, SSD_HEADS), 0.1),
        "ssd_norm_w": 1.0 + nrm((DEPTH, D_INNER), 0.05),
        "w_ssd_proj": nrm((DEPTH, D_INNER, D_MODEL), D_INNER ** -0.5),
        "w_out": nrm((DEPTH, D_MODEL, D_MODEL), D_MODEL ** -0.5),
        "norm2_w": 1.0 + nrm((DEPTH, D_MODEL), 0.05),
        "w_ffn_in": nrm((DEPTH, D_MODEL, 2 * D_FF), D_MODEL ** -0.5),
        "w_ffn_out": nrm((DEPTH, D_FF, D_MODEL), D_FF ** -0.5),
        "final_norm_w": 1.0 + nrm((D_MODEL,), 0.05),
    }


def reference(x_prompt, x_sample, c_prompt, c_sample, state_pool, state_conv, state_ssm, w_ada, b_ada, norm1_w,
              w_in, w_pool, pool_scale, conv_w, conv_b, dt_bias, A_log, D_skip, ssd_norm_w, w_ssd_proj, w_out,
              norm2_w, w_ffn_in, w_ffn_out, final_norm_w):
    xp, xs = x_prompt, x_sample
    pool_p, conv_p, ssm_p, pool_s, conv_s, ssm_s = [], [], [], [], [], []
    for l in range(DEPTH):
        lw = (w_ada[l], b_ada[l], norm1_w[l], w_in[l], w_pool[l], pool_scale[l], conv_w[l], conv_b[l],
              dt_bias[l], A_log[l], D_skip[l], ssd_norm_w[l], w_ssd_proj[l], w_out[l], norm2_w[l],
              w_ffn_in[l], w_ffn_out[l])
        zp = jnp.zeros((BATCH, POOL_BUF, D_POOL), xp.dtype)
        zc = jnp.zeros((BATCH, CONV_W - 1, CONV_DIM), xp.dtype)
        zs = jnp.zeros((BATCH, SSD_HEADS, SSD_HEADDIM, D_STATE), state_ssm.dtype)
        xp, pn, cn, sn = decoder_layer(xp, c_prompt, zp, zc, zs, 0, *lw)
        pool_p.append(pn); conv_p.append(cn); ssm_p.append(sn)
        xs, pn, cn, sn = decoder_layer(xs, c_sample, state_pool[l], state_conv[l], state_ssm[l], PAST_LEN, *lw)
        pool_s.append(pn); conv_s.append(cn); ssm_s.append(sn)
    y_prompt = rmsnorm(xp, final_norm_w)
    y_sample = rmsnorm(xs, final_norm_w)
    return (y_prompt, y_sample, jnp.stack(pool_p), jnp.stack(conv_p), jnp.stack(ssm_p),
            jnp.stack(pool_s), jnp.stack(conv_s), jnp.stack(ssm_s))
```

```python
import functools

import jax
import jax.numpy as jnp
from jax import lax
from jax.experimental import pallas as pl
from jax.experimental.pallas import tpu as pltpu

D_MODEL = 1024
POOL_WINDOWS = (2, 4, 8, 16)
POOL_GW = D_MODEL // len(POOL_WINDOWS)
POOL_BUF = max(POOL_WINDOWS) - 1
D_INNER = 2 * D_MODEL
HEADDIM = 64
HEADS = D_INNER // HEADDIM
GROUPS = 8
HPG = HEADS // GROUPS
D_STATE = 128
CONV_W = 4
CONV_DIM = D_INNER + 2 * GROUPS * D_STATE
CHUNK = 128
D_FF = ((8 * D_MODEL // 3 + 255) // 256) * 256
N_MOD = 6
COL_Z = D_MODEL
COL_XBC = COL_Z + D_INNER
COL_DT = COL_XBC + CONV_DIM
COL_GATE = COL_DT + HEADS
PAST_LEN = 16384
EPS = 1e-6

LANES = 128
GROUP_W = HPG * HEADDIM
POOL_HALO = 16
CONV_HALO = 8
VMEM_LIMIT = 56 * 1024 * 1024

BF16 = jnp.bfloat16
F32 = jnp.float32


def _dot(a, b):
    return jnp.dot(a, b, preferred_element_type=F32)


def _dot_nt(a, b):
    return lax.dot_general(a, b, (((1,), (1,)), ((), ())), preferred_element_type=F32)


def _dot_tn(a, b):
    return lax.dot_general(a, b, (((0,), (0,)), ((), ())), preferred_element_type=F32)


def _silu(x):
    return x * jax.nn.sigmoid(x)


def _rms(x, w):
    return x * lax.rsqrt(jnp.mean(x * x, axis=-1, keepdims=True) + EPS) * w


def _softplus(x):
    return jnp.maximum(x, 0.0) + jnp.log1p(jnp.exp(-jnp.abs(x)))


def _resident(shape):
    nd = len(shape)
    return pl.BlockSpec(shape, lambda *_: (0,) * nd, pipeline_mode=pl.Buffered(1))


def _mod_kernel(c_ref, w_ref, b_ref, o_ref):
    a = _silu(c_ref[...]).astype(BF16)
    o_ref[...] = _dot(a, w_ref[...].astype(BF16)) + b_ref[...]


def _modulation(c, w_ada, b_ada):
    m = c.shape[0]
    n = w_ada.shape[1]
    tn = 512
    return pl.pallas_call(
        _mod_kernel,
        grid=(n // tn,),
        in_specs=[
            pl.BlockSpec((m, D_MODEL), lambda j: (0, 0)),
            pl.BlockSpec((D_MODEL, tn), lambda j: (0, j)),
            pl.BlockSpec((1, tn), lambda j: (0, j)),
        ],
        out_specs=pl.BlockSpec((m, tn), lambda j: (0, j)),
        out_shape=jax.ShapeDtypeStruct((m, n), F32),
        name="adaln_mod",
    )(c, w_ada, b_ada.reshape(1, n))


def _norm_mod(x, sh, sc, w):
    return (_rms(x, w) * (1.0 + sc) + sh).astype(BF16)


def _pool_group_out(pooled, g, wpool_ref, pscale_ref):
    sl = slice(g * POOL_GW, (g + 1) * POOL_GW)
    return _dot(pooled.astype(BF16), wpool_ref[g]) * pscale_ref[:, sl]


def _mix_in_prompt_kernel(x_ref, sh_ref, sc_ref, n1w_ref, wu_ref, wz_ref, wx_ref, wg_ref, wdt_ref,
                          dtb_ref, wpool_ref, pscale_ref, cw_ref, cb_ref,
                          ag_ref, gb_ref, sz_ref, xc_ref, dt_ref, pst_ref, cst_ref,
                          ubuf, xbuf, *, tm):
    i = pl.program_id(1)

    @pl.when(i == 0)
    def _():
        ubuf[0:POOL_HALO, :] = jnp.zeros((POOL_HALO, D_MODEL), F32)
        xbuf[0:CONV_HALO, :] = jnp.zeros((CONV_HALO, CONV_DIM), F32)

    @pl.when(i > 0)
    def _():
        ubuf[0:POOL_HALO, :] = ubuf[tm:tm + POOL_HALO, :]
        xbuf[0:CONV_HALO, :] = xbuf[tm:tm + CONV_HALO, :]

    hb = _norm_mod(x_ref[...], sh_ref[...], sc_ref[...], n1w_ref[...])
    ubuf[POOL_HALO:POOL_HALO + tm, :] = _dot(hb, wu_ref[...])
    sz_ref[...] = _silu(_dot(hb, wz_ref[...]))
    xbuf[CONV_HALO:CONV_HALO + tm, :] = _dot(hb, wx_ref[...])
    gates = jax.nn.sigmoid(_dot(hb, wg_ref[...]))
    gb_ref[...] = gates[:, D_MODEL:]
    dt_ref[...] = _softplus(_dot(hb, wdt_ref[...]) + dtb_ref[...])

    pos = i * tm + lax.broadcasted_iota(jnp.int32, (tm, POOL_GW), 0)
    for g, w in enumerate(POOL_WINDOWS):
        sl = slice(g * POOL_GW, (g + 1) * POOL_GW)
        cur = ubuf[POOL_HALO:POOL_HALO + tm, sl]
        acc = cur
        for k in range(1, w):
            acc = acc + ubuf[POOL_HALO - k:POOL_HALO - k + tm, sl]
        cnt = jnp.minimum(pos + 1, w).astype(F32)
        a = _pool_group_out(acc / cnt - cur, g, wpool_ref, pscale_ref)
        ag_ref[:, sl] = gates[:, sl] * a

    base = CONV_HALO - (CONV_W - 1)
    cchunk = 512
    for c0 in range(0, CONV_DIM, cchunk):
        sl = slice(c0, c0 + cchunk)
        acc = cb_ref[:, sl] + xbuf[base:base + tm, sl] * cw_ref[0:1, sl]
        for k in range(1, CONV_W):
            acc = acc + xbuf[base + k:base + k + tm, sl] * cw_ref[k:k + 1, sl]
        xc_ref[:, sl] = _silu(acc)

    @pl.when(i == pl.num_programs(1) - 1)
    def _():
        pst_ref[...] = ubuf[POOL_HALO + tm - POOL_BUF:POOL_HALO + tm, :]
        cst_ref[...] = xbuf[CONV_HALO + tm - (CONV_W - 1):CONV_HALO + tm, :]


def _mix_in_prompt(x, mod3, lw, tm=256):
    bsz, L, _ = x.shape
    nt = L // tm
    tok = lambda w: pl.BlockSpec((None, tm, w), lambda b, i: (b, i, 0))
    modspec = lambda k: pl.BlockSpec((None, 1, D_MODEL), lambda b, i, k=k: (b, 0, k))
    outs = pl.pallas_call(
        functools.partial(_mix_in_prompt_kernel, tm=tm),
        grid=(bsz, nt),
        in_specs=[
            tok(D_MODEL), modspec(0), modspec(1),
            _resident((1, D_MODEL)),
            _resident((D_MODEL, D_MODEL)), _resident((D_MODEL, D_INNER)),
            _resident((D_MODEL, CONV_DIM)), _resident((D_MODEL, 2 * D_MODEL)),
            _resident((D_MODEL, LANES)), _resident((1, LANES)),
            _resident((len(POOL_WINDOWS), POOL_GW, POOL_GW)), _resident((1, D_MODEL)),
            _resident((CONV_W, CONV_DIM)), _resident((1, CONV_DIM)),
        ],
        out_specs=[
            tok(D_MODEL), tok(D_MODEL), tok(D_INNER), tok(CONV_DIM), tok(LANES),
            pl.BlockSpec((None, POOL_BUF, D_MODEL), lambda b, i: (b, 0, 0)),
            pl.BlockSpec((None, CONV_W - 1, CONV_DIM), lambda b, i: (b, 0, 0)),
        ],
        out_shape=[
            jax.ShapeDtypeStruct((bsz, L, D_MODEL), F32),
            jax.ShapeDtypeStruct((bsz, L, D_MODEL), F32),
            jax.ShapeDtypeStruct((bsz, L, D_INNER), F32),
            jax.ShapeDtypeStruct((bsz, L, CONV_DIM), F32),
            jax.ShapeDtypeStruct((bsz, L, LANES), F32),
            jax.ShapeDtypeStruct((bsz, POOL_BUF, D_MODEL), F32),
            jax.ShapeDtypeStruct((bsz, CONV_W - 1, CONV_DIM), F32),
        ],
        scratch_shapes=[
            pltpu.VMEM((POOL_HALO + tm, D_MODEL), F32),
            pltpu.VMEM((CONV_HALO + tm, CONV_DIM), F32),
        ],
        compiler_params=pltpu.CompilerParams(
            dimension_semantics=("arbitrary", "arbitrary"), vmem_limit_bytes=VMEM_LIMIT),
        name="mix_in_prompt",
    )(x, mod3, mod3, lw["n1w"], lw["wu"], lw["wz"], lw["wx"], lw["wg"], lw["wdt"], lw["dtb"],
      lw["wpool"], lw["pscale"], lw["cw"], lw["cb"])
    return outs


def _mixer_tail(y, sz, ag, gb, x, g1, nw, wproj_ref, wout_ref):
    yn = _rms(y * sz, nw).astype(BF16)
    b_out = _dot(yn, wproj_ref[...])
    mix = (ag + gb * b_out).astype(BF16)
    return x + g1 * _dot(mix, wout_ref[...])


def _head_pair_expand(v, h0):
    rows = v.shape[0]
    lane = lax.broadcasted_iota(jnp.int32, (rows, LANES), 1)
    lo = jnp.broadcast_to(v[:, h0:h0 + 1], (rows, LANES))
    hi = jnp.broadcast_to(v[:, h0 + 1:h0 + 2], (rows, LANES))
    return jnp.where(lane < HEADDIM, lo, hi)


def _group_expand(v, g):
    return jnp.concatenate(
        [_head_pair_expand(v, HPG * g + 2 * t) for t in range(GROUP_W // LANES)], axis=1)


def _ssd_prompt_kernel(xc_ref, dt_ref, sz_ref, ag_ref, gb_ref, x_ref, g1_ref, alog_ref, dexp_ref,
                       nw_ref, wproj_ref, wout_ref,
                       x1_ref, hout_ref, h_scr, y_scr):
    c = pl.program_id(1)

    @pl.when(c == 0)
    def _():
        h_scr[...] = jnp.zeros_like(h_scr)

    q = CHUNK
    row = lax.broadcasted_iota(jnp.int32, (q, q), 0)
    lane = lax.broadcasted_iota(jnp.int32, (q, q), 1)
    head_ok = lane < HEADS
    causal = row >= lane

    dt = jnp.where(head_ok, dt_ref[...], 0.0)
    a_neg = jnp.where(head_ok[0:1, :], -jnp.exp(alog_ref[...]), 0.0)
    d_a = dt * a_neg
    acs = lax.dot_general(causal.astype(F32), d_a, (((1,), (0,)), ((), ())),
                          precision=lax.Precision.HIGHEST, preferred_element_type=F32)
    acs_t = acs.T
    dt_t = dt.T
    acs_last = acs[q - 1:q, :]
    e_acs = jnp.exp(acs)
    w_state = dt * jnp.exp(acs_last - acs)

    lane_g = lax.broadcasted_iota(jnp.int32, (q, GROUP_W), 1)
    for g in range(GROUPS):
        b_g = xc_ref[:, D_INNER + g * D_STATE:D_INNER + (g + 1) * D_STATE].astype(BF16)
        c_g = xc_ref[:, D_INNER + GROUPS * D_STATE + g * D_STATE:
                     D_INNER + GROUPS * D_STATE + (g + 1) * D_STATE].astype(BF16)
        x_g = xc_ref[:, g * GROUP_W:(g + 1) * GROUP_W]
        cb = _dot_nt(c_g, b_g)
        scores = []
        blocks = []
        for r in range(HPG):
            h = HPG * g + r
            seg = acs[:, h:h + 1] - acs_t[h:h + 1, :]
            decay = jnp.exp(jnp.where(causal, seg, -jnp.inf))
            scores.append((cb * decay * dt_t[h:h + 1, :]).astype(BF16))
            in_head = (lane_g >= r * HEADDIM) & (lane_g < (r + 1) * HEADDIM)
            blocks.append(jnp.where(in_head, x_g, 0.0).astype(BF16))
        y_diag = _dot(jnp.concatenate(scores, axis=1), jnp.concatenate(blocks, axis=0))
        h_prev = h_scr[g * GROUP_W:(g + 1) * GROUP_W, :]
        y_off = _dot_nt(c_g, h_prev.astype(BF16)) * _group_expand(e_acs, g)
        y_scr[:, g * GROUP_W:(g + 1) * GROUP_W] = (
            y_diag + y_off + dexp_ref[:, g * GROUP_W:(g + 1) * GROUP_W] * x_g)
        xw = (x_g * _group_expand(w_state, g)).astype(BF16)
        st = _dot_tn(xw, b_g)
        cdec = jnp.concatenate(
            [jnp.broadcast_to(jnp.exp(acs_t[HPG * g + r:HPG * g + r + 1, q - 1:q]),
                              (HEADDIM, D_STATE)) for r in range(HPG)], axis=0)
        h_scr[g * GROUP_W:(g + 1) * GROUP_W, :] = h_prev * cdec + st

    x1_ref[...] = _mixer_tail(y_scr[...], sz_ref[...], ag_ref[...], gb_ref[...], x_ref[...],
                              g1_ref[...], nw_ref[...], wproj_ref, wout_ref)

    @pl.when(c == pl.num_programs(1) - 1)
    def _():
        hout_ref[...] = h_scr[...]


def _ssd_prompt(xc, dt, sz, ag, gb, x, mod3, lw):
    bsz, L, _ = x.shape
    nc = L // CHUNK
    tok = lambda w: pl.BlockSpec((None, CHUNK, w), lambda b, c: (b, c, 0))
    return pl.pallas_call(
        _ssd_prompt_kernel,
        grid=(bsz, nc),
        in_specs=[
            tok(CONV_DIM), tok(LANES), tok(D_INNER), tok(D_MODEL), tok(D_MODEL), tok(D_MODEL),
            pl.BlockSpec((None, 1, D_MODEL), lambda b, c: (b, 0, 2)),
            _resident((1, LANES)), _resident((1, D_INNER)), _resident((1, D_INNER)),
            _resident((D_INNER, D_MODEL)), _resident((D_MODEL, D_MODEL)),
        ],
        out_specs=[
            tok(D_MODEL),
            pl.BlockSpec((None, HEADS * HEADDIM, D_STATE), lambda b, c: (b, 0, 0)),
        ],
        out_shape=[
            jax.ShapeDtypeStruct((bsz, L, D_MODEL), F32),
            jax.ShapeDtypeStruct((bsz, HEADS * HEADDIM, D_STATE), F32),
        ],
        scratch_shapes=[
            pltpu.VMEM((HEADS * HEADDIM, D_STATE), F32),
            pltpu.VMEM((CHUNK, D_INNER), F32),
        ],
        compiler_params=pltpu.CompilerParams(
            dimension_semantics=("arbitrary", "arbitrary"), vmem_limit_bytes=VMEM_LIMIT),
        name="ssd_prompt",
    )(xc, dt, sz, ag, gb, x, mod3, lw["alog"], lw["dexp"], lw["ssd_nw"], lw["wproj"], lw["wout"])


def _ffn_kernel(x_ref, sh_ref, sc_ref, g2_ref, n2w_ref, win_ref, wout_ref, fnw_ref, o_ref, act_scr,
                *, final):
    x = x_ref[...]
    hb = _norm_mod(x, sh_ref[...], sc_ref[...], n2w_ref[...])
    cw = 256
    for c0 in range(0, D_FF, cw):
        gt = _dot(hb, win_ref[:, c0:c0 + cw])
        up = _dot(hb, win_ref[:, D_FF + c0:D_FF + c0 + cw])
        act_scr[:, c0:c0 + cw] = (_silu(gt) * up).astype(BF16)
    x2 = x + g2_ref[...] * _dot(act_scr[...], wout_ref[...])
    o_ref[...] = _rms(x2, fnw_ref[...]) if final else x2


def _ffn(x2d, mods, lw, fnw, final, tm, rows_per_mod):
    t = x2d.shape[0]
    mod_arr, modspec = mods
    return pl.pallas_call(
        functools.partial(_ffn_kernel, final=final),
        grid=(t // tm,),
        in_specs=[
            pl.BlockSpec((tm, D_MODEL), lambda i: (i, 0)),
            modspec(3), modspec(4), modspec(5),
            _resident((1, D_MODEL)),
            _resident((D_MODEL, 2 * D_FF)), _resident((D_FF, D_MODEL)),
            _resident((1, D_MODEL)),
        ],
        out_specs=pl.BlockSpec((tm, D_MODEL), lambda i: (i, 0)),
        out_shape=jax.ShapeDtypeStruct((t, D_MODEL), F32),
        scratch_shapes=[pltpu.VMEM((tm, D_FF), BF16)],
        compiler_params=pltpu.CompilerParams(
            dimension_semantics=("arbitrary",), vmem_limit_bytes=VMEM_LIMIT),
        name="ffn",
    )(x2d, mod_arr, mod_arr, mod_arr, lw["n2w"], lw["wffn_in"], lw["wffn_out"], fnw)


def _mix_in_sample_kernel(x_ref, sh_ref, sc_ref, n1w_ref, wu_ref, wz_ref, wx_ref, wg_ref, wdt_ref,
                          dtb_ref, wpool_ref, pscale_ref, cw_ref, cb_ref, spool_ref, sconv_ref,
                          ag_ref, gb_ref, sz_ref, xc_ref, dt_ref, pnew_ref, cnew_ref):
    hb = _norm_mod(x_ref[...], sh_ref[...], sc_ref[...], n1w_ref[...])
    u = _dot(hb, wu_ref[...])
    sz_ref[...] = _silu(_dot(hb, wz_ref[...]))
    xbc = _dot(hb, wx_ref[...])
    gates = jax.nn.sigmoid(_dot(hb, wg_ref[...]))
    gb_ref[...] = gates[:, D_MODEL:]
    dt_ref[...] = _softplus(_dot(hb, wdt_ref[...]) + dtb_ref[...])

    for g, w in enumerate(POOL_WINDOWS):
        sl = slice(g * POOL_GW, (g + 1) * POOL_GW)
        cur = u[:, sl]
        acc = cur
        for k in range(1, w):
            j = POOL_BUF - k
            acc = acc + spool_ref[:, j * D_MODEL + g * POOL_GW:j * D_MODEL + (g + 1) * POOL_GW]
        cnt = float(min(PAST_LEN + 1, w))
        a = _pool_group_out(acc / cnt - cur, g, wpool_ref, pscale_ref)
        ag_ref[:, sl] = gates[:, sl] * a
    pnew_ref[:, 0:(POOL_BUF - 1) * D_MODEL] = spool_ref[:, D_MODEL:POOL_BUF * D_MODEL]
    pnew_ref[:, (POOL_BUF - 1) * D_MODEL:] = u

    acc = cb_ref[...] + xbc * cw_ref[CONV_W - 1:CONV_W, :]
    for k in range(CONV_W - 1):
        acc = acc + sconv_ref[:, k * CONV_DIM:(k + 1) * CONV_DIM] * cw_ref[k:k + 1, :]
    xc_ref[...] = _silu(acc)
    cnew_ref[:, 0:(CONV_W - 2) * CONV_DIM] = sconv_ref[:, CONV_DIM:(CONV_W - 1) * CONV_DIM]
    cnew_ref[:, (CONV_W - 2) * CONV_DIM:] = xbc


def _mix_in_sample(x2d, mod_s, lw, spool2d, sconv2d):
    n = x2d.shape[0]
    full = lambda r, c: pl.BlockSpec((r, c), lambda i: (0, 0), pipeline_mode=pl.Buffered(1))
    modspec = lambda k: pl.BlockSpec((n, D_MODEL), lambda i, k=k: (0, k),
                                     pipeline_mode=pl.Buffered(1))
    return pl.pallas_call(
        _mix_in_sample_kernel,
        grid=(1,),
        in_specs=[
            full(n, D_MODEL), modspec(0), modspec(1),
            _resident((1, D_MODEL)),
            _resident((D_MODEL, D_MODEL)), _resident((D_MODEL, D_INNER)),
            _resident((D_MODEL, CONV_DIM)), _resident((D_MODEL, 2 * D_MODEL)),
            _resident((D_MODEL, LANES)), _resident((1, LANES)),
            _resident((len(POOL_WINDOWS), POOL_GW, POOL_GW)), _resident((1, D_MODEL)),
            _resident((CONV_W, CONV_DIM)), _resident((1, CONV_DIM)),
            full(n, POOL_BUF * D_MODEL), full(n, (CONV_W - 1) * CONV_DIM),
        ],
        out_specs=[
            full(n, D_MODEL), full(n, D_MODEL), full(n, D_INNER), full(n, CONV_DIM), full(n, LANES),
            full(n, POOL_BUF * D_MODEL), full(n, (CONV_W - 1) * CONV_DIM),
        ],
        out_shape=[
            jax.ShapeDtypeStruct((n, D_MODEL), F32),
            jax.ShapeDtypeStruct((n, D_MODEL), F32),
            jax.ShapeDtypeStruct((n, D_INNER), F32),
            jax.ShapeDtypeStruct((n, CONV_DIM), F32),
            jax.ShapeDtypeStruct((n, LANES), F32),
            jax.ShapeDtypeStruct((n, POOL_BUF * D_MODEL), F32),
            jax.ShapeDtypeStruct((n, (CONV_W - 1) * CONV_DIM), F32),
        ],
        compiler_params=pltpu.CompilerParams(
            dimension_semantics=("arbitrary",), vmem_limit_bytes=VMEM_LIMIT),
        name="mix_in_sample",
    )(x2d, mod_s, mod_s, lw["n1w"], lw["wu"], lw["wz"], lw["wx"], lw["wg"], lw["wdt"], lw["dtb"],
      lw["wpool"], lw["pscale"], lw["cw"], lw["cb"], spool2d, sconv2d)


def _ssm_step_kernel(h_ref, xc_ref, dt_ref, alog_ref, dexp_ref, hout_ref, y_ref, *, bb):
    lane = lax.broadcasted_iota(jnp.int32, (bb, LANES), 1)
    head_ok = lane < HEADS
    dt = jnp.where(head_ok, dt_ref[...], 0.0)
    a_neg = jnp.where(head_ok[0:1, :], -jnp.exp(alog_ref[...]), 0.0)
    dec = jnp.exp(dt * a_neg)
    xs = xc_ref[:, 0:D_INNER]
    dt_exp = jnp.concatenate([_group_expand(dt, g) for g in range(GROUPS)], axis=1)
    pad = jnp.zeros((bb, D_INNER), F32)
    xdt = jnp.concatenate([xs * dt_exp, pad], axis=0).astype(BF16)
    bm = xc_ref[:, D_INNER:D_INNER + GROUPS * D_STATE]
    cm = xc_ref[:, D_INNER + GROUPS * D_STATE:]
    rowid = lax.broadcasted_iota(jnp.int32, (2 * bb, D_STATE), 0)
    rowid_y = lax.broadcasted_iota(jnp.int32, (2 * bb, GROUP_W), 0)
    padn = jnp.zeros((bb, D_STATE), F32)

    y_acc = [jnp.zeros((2 * bb, GROUP_W), F32) for _ in range(GROUPS)]
    for j in range(bb):
        for g in range(GROUPS):
            rs = slice(g * GROUP_W, (g + 1) * GROUP_W)
            ns = slice(g * D_STATE, (g + 1) * D_STATE)
            h_g = h_ref[j, rs, :]
            dcol = jnp.concatenate(
                [jnp.broadcast_to(dec[j:j + 1, HPG * g + r:HPG * g + r + 1], (HEADDIM, D_STATE))
                 for r in range(HPG)], axis=0)
            b16 = jnp.concatenate([bm[:, ns], padn], axis=0)
            b_j = jnp.where(rowid == j, b16, 0.0).astype(BF16)
            new = h_g * dcol + _dot_tn(xdt[:, rs], b_j)
            hout_ref[j, rs, :] = new
            c16 = jnp.concatenate([cm[:, ns], padn], axis=0).astype(BF16)
            y_all = _dot_nt(c16, new.astype(BF16))
            y_acc[g] = y_acc[g] + jnp.where(rowid_y == j, y_all, 0.0)
    y = jnp.concatenate(y_acc, axis=1)[0:bb, :]
    y_ref[...] = y + dexp_ref[...] * xs


def _ssm_step(h3, xc, dt, lw, bb=8):
    n = h3.shape[0]
    return pl.pallas_call(
        functools.partial(_ssm_step_kernel, bb=bb),
        grid=(n // bb,),
        in_specs=[
            pl.BlockSpec((bb, HEADS * HEADDIM, D_STATE), lambda i: (i, 0, 0)),
            pl.BlockSpec((bb, CONV_DIM), lambda i: (i, 0)),
            pl.BlockSpec((bb, LANES), lambda i: (i, 0)),
            _resident((1, LANES)), _resident((1, D_INNER)),
        ],
        out_specs=[
            pl.BlockSpec((bb, HEADS * HEADDIM, D_STATE), lambda i: (i, 0, 0)),
            pl.BlockSpec((bb, D_INNER), lambda i: (i, 0)),
        ],
        out_shape=[
            jax.ShapeDtypeStruct(h3.shape, F32),
            jax.ShapeDtypeStruct((n, D_INNER), F32),
        ],
        compiler_params=pltpu.CompilerParams(
            dimension_semantics=("arbitrary",), vmem_limit_bytes=VMEM_LIMIT),
        name="ssm_step",
    )(h3, xc, dt, lw["alog"], lw["dexp"])


def _mix_out_sample_kernel(y_ref, sz_ref, ag_ref, gb_ref, x_ref, g1_ref, nw_ref, wproj_ref, wout_ref,
                           o_ref):
    o_ref[...] = _mixer_tail(y_ref[...], sz_ref[...], ag_ref[...], gb_ref[...], x_ref[...],
                             g1_ref[...], nw_ref[...], wproj_ref, wout_ref)


def _mix_out_sample(y, sz, ag, gb, x2d, mod_s, lw):
    n = x2d.shape[0]
    full = lambda c: pl.BlockSpec((n, c), lambda i: (0, 0))
    return pl.pallas_call(
        _mix_out_sample_kernel,
        grid=(1,),
        in_specs=[
            full(D_INNER), full(D_INNER), full(D_MODEL), full(D_MODEL), full(D_MODEL),
            pl.BlockSpec((n, D_MODEL), lambda i: (0, 2)),
            _resident((1, D_INNER)), _resident((D_INNER, D_MODEL)), _resident((D_MODEL, D_MODEL)),
        ],
        out_specs=full(D_MODEL),
        out_shape=jax.ShapeDtypeStruct((n, D_MODEL), F32),
        compiler_params=pltpu.CompilerParams(
            dimension_semantics=("arbitrary",), vmem_limit_bytes=VMEM_LIMIT),
        name="mix_out_sample",
    )(y, sz, ag, gb, x2d, mod_s, lw["ssd_nw"], lw["wproj"], lw["wout"])


def _layer_weights(l, w_in, w_pool, pool_scale, conv_w, conv_b, dt_bias, A_log, D_skip, ssd_norm_w,
                   w_ssd_proj, w_out, norm1_w, norm2_w, w_ffn_in, w_ffn_out):
    wi = w_in[l]
    lane_pad = LANES - HEADS
    return dict(
        n1w=norm1_w[l].reshape(1, D_MODEL),
        wu=wi[:, :COL_Z].astype(BF16),
        wz=wi[:, COL_Z:COL_XBC].astype(BF16),
        wx=wi[:, COL_XBC:COL_DT].astype(BF16),
        wdt=jnp.pad(wi[:, COL_DT:COL_GATE], ((0, 0), (0, lane_pad))).astype(BF16),
        wg=wi[:, COL_GATE:].astype(BF16),
        dtb=jnp.pad(dt_bias[l], (0, lane_pad)).reshape(1, LANES),
        wpool=w_pool[l].astype(BF16),
        pscale=pool_scale[l].reshape(1, D_MODEL),
        cw=conv_w[l],
        cb=conv_b[l].reshape(1, CONV_DIM),
        alog=jnp.pad(A_log[l], (0, lane_pad)).reshape(1, LANES),
        dexp=jnp.repeat(D_skip[l], HEADDIM).reshape(1, D_INNER),
        ssd_nw=ssd_norm_w[l].reshape(1, D_INNER),
        wproj=w_ssd_proj[l].astype(BF16),
        wout=w_out[l].astype(BF16),
        n2w=norm2_w[l].reshape(1, D_MODEL),
        wffn_in=w_ffn_in[l].astype(BF16),
        wffn_out=w_ffn_out[l].astype(BF16),
    )


def kernel(x_prompt, x_sample, c_prompt, c_sample, state_pool, state_conv, state_ssm, w_ada, b_ada, norm1_w, w_in, w_pool, pool_scale, conv_w, conv_b, dt_bias, A_log, D_skip, ssd_norm_w, w_ssd_proj, w_out, norm2_w, w_ffn_in, w_ffn_out, final_norm_w):
    depth = w_in.shape[0]
    bsz, L, _ = x_prompt.shape
    nsmp = x_sample.shape[0]
    xp = x_prompt
    xs = x_sample.reshape(nsmp, D_MODEL)
    c_all = jnp.concatenate([c_prompt, c_sample], axis=0)
    fnw = final_norm_w.reshape(1, D_MODEL)
    pool_p, conv_p, ssm_p, pool_s, conv_s, ssm_s = [], [], [], [], [], []
    for l in range(depth):
        final = l == depth - 1
        lw = _layer_weights(l, w_in, w_pool, pool_scale, conv_w, conv_b, dt_bias, A_log, D_skip,
                            ssd_norm_w, w_ssd_proj, w_out, norm1_w, norm2_w, w_ffn_in, w_ffn_out)
        mod = _modulation(c_all, w_ada[l], b_ada[l])
        mod_p = mod[:bsz].reshape(bsz, 1, N_MOD * D_MODEL)
        mod_s = mod[bsz:]

        ag, gb, sz, xc, dt, pst, cst = _mix_in_prompt(xp, mod_p, lw)
        x1, hst = _ssd_prompt(xc, dt, sz, ag, gb, xp, mod_p, lw)
        tm = 256
        per = L // tm
        p_mods = (mod_p, lambda k: pl.BlockSpec((None, 1, D_MODEL), lambda i, k=k: (i // per, 0, k)))
        xp = _ffn(x1.reshape(bsz * L, D_MODEL), p_mods, lw, fnw, final, tm, per).reshape(bsz, L, D_MODEL)
        pool_p.append(pst)
        conv_p.append(cst)
        ssm_p.append(hst.reshape(bsz, HEADS, HEADDIM, D_STATE))

        spool2d = state_pool[l].reshape(nsmp, POOL_BUF * D_MODEL)
        sconv2d = state_conv[l].reshape(nsmp, (CONV_W - 1) * CONV_DIM)
        ag, gb, sz, xc, dt, pnew, cnew = _mix_in_sample(xs, mod_s, lw, spool2d, sconv2d)
        hnew, y = _ssm_step(state_ssm[l].reshape(nsmp, HEADS * HEADDIM, D_STATE), xc, dt, lw)
        x1s = _mix_out_sample(y, sz, ag, gb, xs, mod_s, lw)
        s_mods = (mod_s, lambda k: pl.BlockSpec((nsmp, D_MODEL), lambda i, k=k: (0, k)))
        xs = _ffn(x1s, s_mods, lw, fnw, final, nsmp, 1)
        pool_s.append(pnew.reshape(nsmp, POOL_BUF, D_MODEL))
        conv_s.append(cnew.reshape(nsmp, CONV_W - 1, CONV_DIM))
        ssm_s.append(hnew.reshape(nsmp, HEADS, HEADDIM, D_STATE))

    return (xp, xs.reshape(nsmp, 1, D_MODEL), jnp.stack(pool_p), jnp.stack(conv_p), jnp.stack(ssm_p),
            jnp.stack(pool_s), jnp.stack(conv_s), jnp.stack(ssm_s))
```

```python
import functools

import jax
import jax.numpy as jnp
from jax import lax
from jax.experimental import pallas as pl
from jax.experimental.pallas import tpu as pltpu

D_MODEL = 1024
POOL_WINDOWS = (2, 4, 8, 16)
POOL_GW = D_MODEL // len(POOL_WINDOWS)
POOL_BUF = max(POOL_WINDOWS) - 1
D_INNER = 2 * D_MODEL
HEADDIM = 64
HEADS = D_INNER // HEADDIM
GROUPS = 8
HPG = HEADS // GROUPS
D_STATE = 128
CONV_W = 4
CONV_DIM = D_INNER + 2 * GROUPS * D_STATE
CHUNK = 128
D_FF = ((8 * D_MODEL // 3 + 255) // 256) * 256
N_MOD = 6
COL_Z = D_MODEL
COL_XBC = COL_Z + D_INNER
COL_DT = COL_XBC + CONV_DIM
COL_GATE = COL_DT + HEADS
PAST_LEN = 16384
EPS = 1e-6

LANES = 128
GROUP_W = HPG * HEADDIM
POOL_HALO = 16
CONV_HALO = 8
CONV_ROWS = 64
VMEM_LIMIT = 56 * 1024 * 1024

BF16 = jnp.bfloat16
F32 = jnp.float32


def _dot(a, b):
    return jnp.dot(a, b, preferred_element_type=F32)


def _dot_nt(a, b):
    return lax.dot_general(a, b, (((1,), (1,)), ((), ())), preferred_element_type=F32)


def _dot_tn(a, b):
    return lax.dot_general(a, b, (((0,), (0,)), ((), ())), preferred_element_type=F32)


def _silu(x):
    h = 0.5 * x
    return h * jnp.tanh(h) + h


def _rms(x, w):
    return x * lax.rsqrt(jnp.mean(x * x, axis=-1, keepdims=True) + EPS) * w


def _softplus(x):
    return jnp.maximum(x, 0.0) + jnp.log1p(jnp.exp(-jnp.abs(x)))


def _resident(shape):
    nd = len(shape)
    return pl.BlockSpec(shape, lambda *_: (0,) * nd, pipeline_mode=pl.Buffered(1))


def _mod_kernel(c_ref, w_ref, b_ref, o_ref):
    a = _silu(c_ref[...]).astype(BF16)
    o_ref[...] = _dot(a, w_ref[...].astype(BF16)) + b_ref[...]


def _modulation(c, w_ada, b_ada):
    m = c.shape[0]
    n = w_ada.shape[1]
    tn = 512
    return pl.pallas_call(
        _mod_kernel,
        grid=(n // tn,),
        in_specs=[
            pl.BlockSpec((m, D_MODEL), lambda j: (0, 0)),
            pl.BlockSpec((D_MODEL, tn), lambda j: (0, j)),
            pl.BlockSpec((1, tn), lambda j: (0, j)),
        ],
        out_specs=pl.BlockSpec((m, tn), lambda j: (0, j)),
        out_shape=jax.ShapeDtypeStruct((m, n), F32),
        name="adaln_mod",
    )(c, w_ada, b_ada.reshape(1, n))


def _norm_mod(x, sh, sc, w):
    return (_rms(x, w) * (1.0 + sc) + sh).astype(BF16)


def _pool_group_out(pooled, g, wpool_ref, pscale_ref):
    sl = slice(g * POOL_GW, (g + 1) * POOL_GW)
    return _dot(pooled.astype(BF16), wpool_ref[g]) * pscale_ref[:, sl]


def _mix_in_prompt_kernel(x_ref, sh_ref, sc_ref, n1w_ref, wu_ref, wz_ref, wx_ref, wg_ref, wdt_ref,
                          dtb_ref, wpool_ref, pscale_ref, cw_ref, cb_ref,
                          ag_ref, gb_ref, sz_ref, xc_ref, dt_ref, pst_ref, cst_ref,
                          ubuf, xbuf, *, tm):
    i = pl.program_id(1)

    @pl.when(i == 0)
    def _():
        ubuf[0:POOL_HALO, :] = jnp.zeros((POOL_HALO, D_MODEL), F32)
        xbuf[0:CONV_HALO, :] = jnp.zeros((CONV_HALO, CONV_DIM), F32)

    @pl.when(i > 0)
    def _():
        ubuf[0:POOL_HALO, :] = ubuf[tm:tm + POOL_HALO, :]
        xbuf[0:CONV_HALO, :] = xbuf[tm:tm + CONV_HALO, :]

    hb = _norm_mod(x_ref[...], sh_ref[...], sc_ref[...], n1w_ref[...])
    dt_ref[...] = _softplus(_dot(hb, wdt_ref[...]) + dtb_ref[...])

    pos = i * tm + lax.broadcasted_iota(jnp.int32, (tm, POOL_GW), 0)

    def pool_group(g):
        w = POOL_WINDOWS[g]
        sl = slice(g * POOL_GW, (g + 1) * POOL_GW)
        ubuf[POOL_HALO:POOL_HALO + tm, sl] = _dot(hb, wu_ref[:, sl])
        full = ubuf[:, sl]
        acc = full
        span = 1
        while span < w:
            acc = acc + pltpu.roll(acc, span, axis=0)
            span *= 2
        cur = full[POOL_HALO:, :]
        cnt = jnp.minimum(pos + 1, w).astype(F32)
        a = _pool_group_out(acc[POOL_HALO:, :] / cnt - cur, g, wpool_ref, pscale_ref)
        ag_ref[:, sl] = jax.nn.sigmoid(_dot(hb, wg_ref[:, sl])) * a

    cchunk = 1024

    def gate_b_chunk(c0):
        gb_ref[:, c0:c0 + cchunk] = jax.nn.sigmoid(
            _dot(hb, wg_ref[:, D_MODEL + c0:D_MODEL + c0 + cchunk]))

    def z_chunk(c0):
        sz_ref[:, c0:c0 + cchunk] = _silu(_dot(hb, wz_ref[:, c0:c0 + cchunk]))

    def conv_chunk(c0):
        xbuf[CONV_HALO:CONV_HALO + tm, c0:c0 + cchunk] = _dot(hb, wx_ref[:, c0:c0 + cchunk])
        for l0 in range(c0, c0 + cchunk, LANES):
            sl = slice(l0, l0 + LANES)
            for r0 in range(0, tm, CONV_ROWS):
                src = xbuf[r0:r0 + CONV_HALO + CONV_ROWS, sl]
                acc = cb_ref[:, sl] + src[CONV_HALO:, :] * cw_ref[CONV_W - 1:CONV_W, sl]
                for k in range(CONV_W - 1):
                    back = CONV_W - 1 - k
                    acc = acc + pltpu.roll(src, back, axis=0)[CONV_HALO:, :] * cw_ref[k:k + 1, sl]
                xc_ref[r0:r0 + CONV_ROWS, sl] = _silu(acc)

    light = ([functools.partial(pool_group, g) for g in range(len(POOL_WINDOWS))]
             + [functools.partial(gate_b_chunk, c0) for c0 in range(0, D_MODEL, cchunk)]
             + [functools.partial(z_chunk, c0) for c0 in range(0, D_INNER, cchunk)])
    heavy = [functools.partial(conv_chunk, c0) for c0 in range(0, CONV_DIM, cchunk)]
    for k in range(max(len(light), len(heavy))):
        if k < len(light):
            light[k]()
        if k < len(heavy):
            heavy[k]()

    @pl.when(i == pl.num_programs(1) - 1)
    def _():
        pst_ref[...] = ubuf[POOL_HALO + tm - POOL_BUF:POOL_HALO + tm, :]
        cst_ref[...] = xbuf[CONV_HALO + tm - (CONV_W - 1):CONV_HALO + tm, :]


def _mix_in_prompt(x, mod3, lw, tm=256):
    bsz, L, _ = x.shape
    nt = L // tm
    tok = lambda w: pl.BlockSpec((None, tm, w), lambda b, i: (b, i, 0))
    modspec = lambda k: pl.BlockSpec((None, 1, D_MODEL), lambda b, i, k=k: (b, 0, k))
    outs = pl.pallas_call(
        functools.partial(_mix_in_prompt_kernel, tm=tm),
        grid=(bsz, nt),
        in_specs=[
            tok(D_MODEL), modspec(0), modspec(1),
            _resident((1, D_MODEL)),
            _resident((D_MODEL, D_MODEL)), _resident((D_MODEL, D_INNER)),
            _resident((D_MODEL, CONV_DIM)), _resident((D_MODEL, 2 * D_MODEL)),
            _resident((D_MODEL, LANES)), _resident((1, LANES)),
            _resident((len(POOL_WINDOWS), POOL_GW, POOL_GW)), _resident((1, D_MODEL)),
            _resident((CONV_W, CONV_DIM)), _resident((1, CONV_DIM)),
        ],
        out_specs=[
            tok(D_MODEL), tok(D_MODEL), tok(D_INNER), tok(CONV_DIM), tok(LANES),
            pl.BlockSpec((None, POOL_BUF, D_MODEL), lambda b, i: (b, 0, 0)),
            pl.BlockSpec((None, CONV_W - 1, CONV_DIM), lambda b, i: (b, 0, 0)),
        ],
        out_shape=[
            jax.ShapeDtypeStruct((bsz, L, D_MODEL), F32),
            jax.ShapeDtypeStruct((bsz, L, D_MODEL), F32),
            jax.ShapeDtypeStruct((bsz, L, D_INNER), F32),
            jax.ShapeDtypeStruct((bsz, L, CONV_DIM), F32),
            jax.ShapeDtypeStruct((bsz, L, LANES), F32),
            jax.ShapeDtypeStruct((bsz, POOL_BUF, D_MODEL), F32),
            jax.ShapeDtypeStruct((bsz, CONV_W - 1, CONV_DIM), F32),
        ],
        scratch_shapes=[
            pltpu.VMEM((POOL_HALO + tm, D_MODEL), F32),
            pltpu.VMEM((CONV_HALO + tm, CONV_DIM), F32),
        ],
        compiler_params=pltpu.CompilerParams(
            dimension_semantics=("arbitrary", "arbitrary"), vmem_limit_bytes=VMEM_LIMIT),
        name="mix_in_prompt",
    )(x, mod3, mod3, lw["n1w"], lw["wu"], lw["wz"], lw["wx"], lw["wg"], lw["wdt"], lw["dtb"],
      lw["wpool"], lw["pscale"], lw["cw"], lw["cb"])
    return outs


def _mixer_tail(y, sz, ag, gb, x, g1, nw, wproj_ref, wout_ref):
    yn = _rms(y * sz, nw).astype(BF16)
    b_out = _dot(yn, wproj_ref[...])
    mix = (ag + gb * b_out).astype(BF16)
    return x + g1 * _dot(mix, wout_ref[...])


def _head_pair_expand(v, h0):
    rows = v.shape[0]
    lane = lax.broadcasted_iota(jnp.int32, (rows, LANES), 1)
    lo = jnp.broadcast_to(v[:, h0:h0 + 1], (rows, LANES))
    hi = jnp.broadcast_to(v[:, h0 + 1:h0 + 2], (rows, LANES))
    return jnp.where(lane < HEADDIM, lo, hi)


def _group_expand(v, g):
    return jnp.concatenate(
        [_head_pair_expand(v, HPG * g + 2 * t) for t in range(GROUP_W // LANES)], axis=1)


def _ssd_prompt_kernel(xc_ref, dt_ref, sz_ref, ag_ref, gb_ref, x_ref, g1_ref, alog_ref, dexp_ref,
                       hexp_ref, nw_ref, wproj_ref, wout_ref,
                       x1_ref, hout_ref, ht_scr, y_scr):
    c = pl.program_id(1)

    @pl.when(c == 0)
    def _():
        ht_scr[...] = jnp.zeros_like(ht_scr)

    q = CHUNK
    row = lax.broadcasted_iota(jnp.int32, (q, q), 0)
    lane = lax.broadcasted_iota(jnp.int32, (q, q), 1)
    causal = row >= lane

    dt = dt_ref[...]
    d_a = dt * (-jnp.exp(alog_ref[...]))
    acs = lax.dot_general(causal.astype(F32), d_a, (((1,), (0,)), ((), ())),
                          precision=lax.Precision.HIGHEST, preferred_element_type=F32)
    acs_t = acs.T
    dt_t = dt.T
    w_t = dt_t * jnp.exp(acs_t[:, q - 1:q] - acs_t)

    e_acs = jnp.exp(acs)
    e_hi = e_acs.astype(BF16).astype(F32)
    e_r1 = e_acs - e_hi
    e_mid = e_r1.astype(BF16).astype(F32)
    e_lo = e_r1 - e_mid
    pieces = jnp.where(lane < HEADS, e_hi,
                       jnp.where(lane < 2 * HEADS, e_mid,
                                 jnp.where(lane < 3 * HEADS, e_lo, 0.0))).astype(BF16)
    e_exp = _dot(pieces, hexp_ref[...])

    c_bf, bt_f, cbs = [], [], []
    for g in range(GROUPS):
        b0 = D_INNER + g * D_STATE
        c0 = D_INNER + GROUPS * D_STATE + g * D_STATE
        bt = xc_ref[:, b0:b0 + D_STATE].T
        cg = xc_ref[:, c0:c0 + D_STATE].astype(BF16)
        c_bf.append(cg)
        bt_f.append(bt)
        cbs.append(_dot(cg, bt.astype(BF16)))

    lane_g = lax.broadcasted_iota(jnp.int32, (q, GROUP_W), 1)
    for g in range(GROUPS):
        gs = slice(g * GROUP_W, (g + 1) * GROUP_W)
        x_g = xc_ref[:, gs]
        scores, btw, blocks = [], [], []
        for r in range(HPG):
            h = HPG * g + r
            seg = acs[:, h:h + 1] - acs_t[h:h + 1, :]
            decay = jnp.exp(jnp.where(causal, seg, -jnp.inf))
            scores.append((cbs[g] * decay * dt_t[h:h + 1, :]).astype(BF16))
            btw.append((bt_f[g] * w_t[h:h + 1, :]).astype(BF16))
            in_head = (lane_g >= r * HEADDIM) & (lane_g < (r + 1) * HEADDIM)
            blocks.append(jnp.where(in_head, x_g, 0.0).astype(BF16))
        lhs = jnp.concatenate([jnp.concatenate(scores, axis=1), jnp.concatenate(btw, axis=1)], axis=0)
        both = _dot(lhs, jnp.concatenate(blocks, axis=0))
        ht_prev = ht_scr[g * D_STATE:(g + 1) * D_STATE, :]
        y_off = _dot(c_bf[g], ht_prev.astype(BF16)) * e_exp[:, gs]
        y_scr[:, gs] = both[0:q, :] + y_off + dexp_ref[:, gs] * x_g
        ht_scr[g * D_STATE:(g + 1) * D_STATE, :] = ht_prev * e_exp[q - 1:q, gs] + both[q:2 * q, :]

    x1_ref[...] = _mixer_tail(y_scr[...], sz_ref[...], ag_ref[...], gb_ref[...], x_ref[...],
                              g1_ref[...], nw_ref[...], wproj_ref, wout_ref)

    @pl.when(c == pl.num_programs(1) - 1)
    def _():
        for g in range(GROUPS):
            hout_ref[g * GROUP_W:(g + 1) * GROUP_W, :] = ht_scr[g * D_STATE:(g + 1) * D_STATE, :].T


def _ssd_prompt(xc, dt, sz, ag, gb, x, mod3, lw):
    bsz, L, _ = x.shape
    nc = L // CHUNK
    tok = lambda w: pl.BlockSpec((None, CHUNK, w), lambda b, c: (b, c, 0))
    return pl.pallas_call(
        _ssd_prompt_kernel,
        grid=(bsz, nc),
        in_specs=[
            tok(CONV_DIM), tok(LANES), tok(D_INNER), tok(D_MODEL), tok(D_MODEL), tok(D_MODEL),
            pl.BlockSpec((None, 1, D_MODEL), lambda b, c: (b, 0, 2)),
            _resident((1, LANES)), _resident((1, D_INNER)), _resident((LANES, D_INNER)),
            _resident((1, D_INNER)),
            _resident((D_INNER, D_MODEL)), _resident((D_MODEL, D_MODEL)),
        ],
        out_specs=[
            tok(D_MODEL),
            pl.BlockSpec((None, HEADS * HEADDIM, D_STATE), lambda b, c: (b, 0, 0)),
        ],
        out_shape=[
            jax.ShapeDtypeStruct((bsz, L, D_MODEL), F32),
            jax.ShapeDtypeStruct((bsz, HEADS * HEADDIM, D_STATE), F32),
        ],
        scratch_shapes=[
            pltpu.VMEM((GROUPS * D_STATE, GROUP_W), F32),
            pltpu.VMEM((CHUNK, D_INNER), F32),
        ],
        compiler_params=pltpu.CompilerParams(
            dimension_semantics=("arbitrary", "arbitrary"), vmem_limit_bytes=VMEM_LIMIT),
        name="ssd_prompt",
    )(xc, dt, sz, ag, gb, x, mod3, lw["alog"], lw["dexp"], lw["hexp"], lw["ssd_nw"], lw["wproj"],
      lw["wout"])


def _ffn_kernel(x_ref, sh_ref, sc_ref, g2_ref, n2w_ref, win_ref, wout_ref, fnw_ref, o_ref, act_scr,
                *, final):
    x = x_ref[...]
    hb = _norm_mod(x, sh_ref[...], sc_ref[...], n2w_ref[...])
    cw = 256
    for c0 in range(0, D_FF, cw):
        gt = _dot(hb, win_ref[:, c0:c0 + cw])
        up = _dot(hb, win_ref[:, D_FF + c0:D_FF + c0 + cw])
        act_scr[:, c0:c0 + cw] = (_silu(gt) * up).astype(BF16)
    x2 = x + g2_ref[...] * _dot(act_scr[...], wout_ref[...])
    o_ref[...] = _rms(x2, fnw_ref[...]) if final else x2


def _ffn(x2d, mods, lw, fnw, final, tm):
    t = x2d.shape[0]
    mod_arr, modspec = mods
    return pl.pallas_call(
        functools.partial(_ffn_kernel, final=final),
        grid=(t // tm,),
        in_specs=[
            pl.BlockSpec((tm, D_MODEL), lambda i: (i, 0)),
            modspec(3), modspec(4), modspec(5),
            _resident((1, D_MODEL)),
            _resident((D_MODEL, 2 * D_FF)), _resident((D_FF, D_MODEL)),
            _resident((1, D_MODEL)),
        ],
        out_specs=pl.BlockSpec((tm, D_MODEL), lambda i: (i, 0)),
        out_shape=jax.ShapeDtypeStruct((t, D_MODEL), F32),
        scratch_shapes=[pltpu.VMEM((tm, D_FF), BF16)],
        compiler_params=pltpu.CompilerParams(
            dimension_semantics=("arbitrary",), vmem_limit_bytes=VMEM_LIMIT),
        name="ffn",
    )(x2d, mod_arr, mod_arr, mod_arr, lw["n2w"], lw["wffn_in"], lw["wffn_out"], fnw)


def _mix_in_sample_kernel(x_ref, sh_ref, sc_ref, n1w_ref, wu_ref, wz_ref, wx_ref, wg_ref, wdt_ref,
                          dtb_ref, wpool_ref, pscale_ref, cw_ref, cb_ref, spool_ref, sconv_ref,
                          ag_ref, gb_ref, sz_ref, xc_ref, dt_ref, u_ref, xraw_ref):
    hb = _norm_mod(x_ref[...], sh_ref[...], sc_ref[...], n1w_ref[...])
    u = _dot(hb, wu_ref[...])
    u_ref[...] = u
    sz_ref[...] = _silu(_dot(hb, wz_ref[...]))
    xbc = _dot(hb, wx_ref[...])
    xraw_ref[...] = xbc
    gates = jax.nn.sigmoid(_dot(hb, wg_ref[...]))
    gb_ref[...] = gates[:, D_MODEL:]
    dt_ref[...] = _softplus(_dot(hb, wdt_ref[...]) + dtb_ref[...])

    for g, w in enumerate(POOL_WINDOWS):
        sl = slice(g * POOL_GW, (g + 1) * POOL_GW)
        cur = u[:, sl]
        acc = cur
        for k in range(1, w):
            acc = acc + spool_ref[POOL_BUF - k, :, sl]
        cnt = float(min(PAST_LEN + 1, w))
        a = _pool_group_out(acc / cnt - cur, g, wpool_ref, pscale_ref)
        ag_ref[:, sl] = gates[:, sl] * a

    acc = cb_ref[...] + xbc * cw_ref[CONV_W - 1:CONV_W, :]
    for k in range(CONV_W - 1):
        acc = acc + sconv_ref[k] * cw_ref[k:k + 1, :]
    xc_ref[...] = _silu(acc)


def _mix_in_sample(x2d, mod_s, lw, spool_t, sconv_t):
    n = x2d.shape[0]
    full = lambda r, c: pl.BlockSpec((r, c), lambda i: (0, 0), pipeline_mode=pl.Buffered(1))
    full3 = lambda a, r, c: pl.BlockSpec((a, r, c), lambda i: (0, 0, 0), pipeline_mode=pl.Buffered(1))
    modspec = lambda k: pl.BlockSpec((n, D_MODEL), lambda i, k=k: (0, k),
                                     pipeline_mode=pl.Buffered(1))
    return pl.pallas_call(
        _mix_in_sample_kernel,
        grid=(1,),
        in_specs=[
            full(n, D_MODEL), modspec(0), modspec(1),
            _resident((1, D_MODEL)),
            _resident((D_MODEL, D_MODEL)), _resident((D_MODEL, D_INNER)),
            _resident((D_MODEL, CONV_DIM)), _resident((D_MODEL, 2 * D_MODEL)),
            _resident((D_MODEL, LANES)), _resident((1, LANES)),
            _resident((len(POOL_WINDOWS), POOL_GW, POOL_GW)), _resident((1, D_MODEL)),
            _resident((CONV_W, CONV_DIM)), _resident((1, CONV_DIM)),
            full3(POOL_BUF, n, D_MODEL), full3(CONV_W - 1, n, CONV_DIM),
        ],
        out_specs=[
            full(n, D_MODEL), full(n, D_MODEL), full(n, D_INNER), full(n, CONV_DIM), full(n, LANES),
            full(n, D_MODEL), full(n, CONV_DIM),
        ],
        out_shape=[
            jax.ShapeDtypeStruct((n, D_MODEL), F32),
            jax.ShapeDtypeStruct((n, D_MODEL), F32),
            jax.ShapeDtypeStruct((n, D_INNER), F32),
            jax.ShapeDtypeStruct((n, CONV_DIM), F32),
            jax.ShapeDtypeStruct((n, LANES), F32),
            jax.ShapeDtypeStruct((n, D_MODEL), F32),
            jax.ShapeDtypeStruct((n, CONV_DIM), F32),
        ],
        compiler_params=pltpu.CompilerParams(
            dimension_semantics=("arbitrary",), vmem_limit_bytes=VMEM_LIMIT),
        name="mix_in_sample",
    )(x2d, mod_s, mod_s, lw["n1w"], lw["wu"], lw["wz"], lw["wx"], lw["wg"], lw["wdt"], lw["dtb"],
      lw["wpool"], lw["pscale"], lw["cw"], lw["cb"], spool_t, sconv_t)


def _ssm_step_kernel(h_ref, xc_ref, dt_ref, alog_ref, dexp_ref, hout_ref, y_ref, *, bb):
    lane = lax.broadcasted_iota(jnp.int32, (bb, LANES), 1)
    head_ok = lane < HEADS
    dt = jnp.where(head_ok, dt_ref[...], 0.0)
    a_neg = jnp.where(head_ok[0:1, :], -jnp.exp(alog_ref[...]), 0.0)
    dec = jnp.exp(dt * a_neg)
    xs = xc_ref[:, 0:D_INNER]
    dt_exp = jnp.concatenate([_group_expand(dt, g) for g in range(GROUPS)], axis=1)
    pad = jnp.zeros((bb, D_INNER), F32)
    xdt = jnp.concatenate([xs * dt_exp, pad], axis=0).astype(BF16)
    bm = xc_ref[:, D_INNER:D_INNER + GROUPS * D_STATE]
    cm = xc_ref[:, D_INNER + GROUPS * D_STATE:]
    rowid = lax.broadcasted_iota(jnp.int32, (2 * bb, D_STATE), 0)
    rowid_y = lax.broadcasted_iota(jnp.int32, (2 * bb, GROUP_W), 0)
    padn = jnp.zeros((bb, D_STATE), F32)

    y_acc = [jnp.zeros((2 * bb, GROUP_W), F32) for _ in range(GROUPS)]
    for j in range(bb):
        for g in range(GROUPS):
            rs = slice(g * GROUP_W, (g + 1) * GROUP_W)
            ns = slice(g * D_STATE, (g + 1) * D_STATE)
            h_g = h_ref[j, rs, :]
            dcol = jnp.concatenate(
                [jnp.broadcast_to(dec[j:j + 1, HPG * g + r:HPG * g + r + 1], (HEADDIM, D_STATE))
                 for r in range(HPG)], axis=0)
            b16 = jnp.concatenate([bm[:, ns], padn], axis=0)
            b_j = jnp.where(rowid == j, b16, 0.0).astype(BF16)
            new = h_g * dcol + _dot_tn(xdt[:, rs], b_j)
            hout_ref[j, rs, :] = new
            c16 = jnp.concatenate([cm[:, ns], padn], axis=0).astype(BF16)
            y_all = _dot_nt(c16, new.astype(BF16))
            y_acc[g] = y_acc[g] + jnp.where(rowid_y == j, y_all, 0.0)
    y = jnp.concatenate(y_acc, axis=1)[0:bb, :]
    y_ref[...] = y + dexp_ref[...] * xs


def _ssm_step(h3, xc, dt, lw, bb=8):
    n = h3.shape[0]
    return pl.pallas_call(
        functools.partial(_ssm_step_kernel, bb=bb),
        grid=(n // bb,),
        in_specs=[
            pl.BlockSpec((bb, HEADS * HEADDIM, D_STATE), lambda i: (i, 0, 0)),
            pl.BlockSpec((bb, CONV_DIM), lambda i: (i, 0)),
            pl.BlockSpec((bb, LANES), lambda i: (i, 0)),
            _resident((1, LANES)), _resident((1, D_INNER)),
        ],
        out_specs=[
            pl.BlockSpec((bb, HEADS * HEADDIM, D_STATE), lambda i: (i, 0, 0)),
            pl.BlockSpec((bb, D_INNER), lambda i: (i, 0)),
        ],
        out_shape=[
            jax.ShapeDtypeStruct(h3.shape, F32),
            jax.ShapeDtypeStruct((n, D_INNER), F32),
        ],
        compiler_params=pltpu.CompilerParams(
            dimension_semantics=("arbitrary",), vmem_limit_bytes=VMEM_LIMIT),
        name="ssm_step",
    )(h3, xc, dt, lw["alog"], lw["dexp"])


def _mix_out_sample_kernel(y_ref, sz_ref, ag_ref, gb_ref, x_ref, g1_ref, nw_ref, wproj_ref, wout_ref,
                           o_ref):
    o_ref[...] = _mixer_tail(y_ref[...], sz_ref[...], ag_ref[...], gb_ref[...], x_ref[...],
                             g1_ref[...], nw_ref[...], wproj_ref, wout_ref)


def _mix_out_sample(y, sz, ag, gb, x2d, mod_s, lw):
    n = x2d.shape[0]
    full = lambda c: pl.BlockSpec((n, c), lambda i: (0, 0))
    return pl.pallas_call(
        _mix_out_sample_kernel,
        grid=(1,),
        in_specs=[
            full(D_INNER), full(D_INNER), full(D_MODEL), full(D_MODEL), full(D_MODEL),
            pl.BlockSpec((n, D_MODEL), lambda i: (0, 2)),
            _resident((1, D_INNER)), _resident((D_INNER, D_MODEL)), _resident((D_MODEL, D_MODEL)),
        ],
        out_specs=full(D_MODEL),
        out_shape=jax.ShapeDtypeStruct((n, D_MODEL), F32),
        compiler_params=pltpu.CompilerParams(
            dimension_semantics=("arbitrary",), vmem_limit_bytes=VMEM_LIMIT),
        name="mix_out_sample",
    )(y, sz, ag, gb, x2d, mod_s, lw["ssd_nw"], lw["wproj"], lw["wout"])


def _layer_weights(l, w_in, w_pool, pool_scale, conv_w, conv_b, dt_bias, A_log, D_skip, ssd_norm_w,
                   w_ssd_proj, w_out, norm1_w, norm2_w, w_ffn_in, w_ffn_out):
    wi = w_in[l]
    rep = LANES // HEADS
    lane_id = jnp.arange(LANES)[:, None]
    col_head = jnp.arange(D_INNER)[None, :] // HEADDIM
    hexp = ((lane_id % HEADS == col_head) & (lane_id < 3 * HEADS)).astype(BF16)
    return dict(
        n1w=norm1_w[l].reshape(1, D_MODEL),
        wu=wi[:, :COL_Z].astype(BF16),
        wz=wi[:, COL_Z:COL_XBC].astype(BF16),
        wx=wi[:, COL_XBC:COL_DT].astype(BF16),
        wdt=jnp.tile(wi[:, COL_DT:COL_GATE], (1, rep)).astype(BF16),
        wg=wi[:, COL_GATE:].astype(BF16),
        dtb=jnp.tile(dt_bias[l], rep).reshape(1, LANES),
        wpool=w_pool[l].astype(BF16),
        pscale=pool_scale[l].reshape(1, D_MODEL),
        cw=conv_w[l],
        cb=conv_b[l].reshape(1, CONV_DIM),
        alog=jnp.tile(A_log[l], rep).reshape(1, LANES),
        hexp=hexp,
        dexp=jnp.repeat(D_skip[l], HEADDIM).reshape(1, D_INNER),
        ssd_nw=ssd_norm_w[l].reshape(1, D_INNER),
        wproj=w_ssd_proj[l].astype(BF16),
        wout=w_out[l].astype(BF16),
        n2w=norm2_w[l].reshape(1, D_MODEL),
        wffn_in=w_ffn_in[l].astype(BF16),
        wffn_out=w_ffn_out[l].astype(BF16),
    )


def kernel(x_prompt, x_sample, c_prompt, c_sample, state_pool, state_conv, state_ssm, w_ada, b_ada, norm1_w, w_in, w_pool, pool_scale, conv_w, conv_b, dt_bias, A_log, D_skip, ssd_norm_w, w_ssd_proj, w_out, norm2_w, w_ffn_in, w_ffn_out, final_norm_w):
    depth = w_in.shape[0]
    bsz, L, _ = x_prompt.shape
    nsmp = x_sample.shape[0]
    xp = x_prompt
    xs = x_sample.reshape(nsmp, D_MODEL)
    c_all = jnp.concatenate([c_prompt, c_sample], axis=0)
    fnw = final_norm_w.reshape(1, D_MODEL)
    pool_p, conv_p, ssm_p, pool_s, conv_s, ssm_s = [], [], [], [], [], []
    for l in range(depth):
        final = l == depth - 1
        lw = _layer_weights(l, w_in, w_pool, pool_scale, conv_w, conv_b, dt_bias, A_log, D_skip,
                            ssd_norm_w, w_ssd_proj, w_out, norm1_w, norm2_w, w_ffn_in, w_ffn_out)
        mod = _modulation(c_all, w_ada[l], b_ada[l])
        mod_p = mod[:bsz].reshape(bsz, 1, N_MOD * D_MODEL)
        mod_s = mod[bsz:]

        ag, gb, sz, xc, dt, pst, cst = _mix_in_prompt(xp, mod_p, lw)
        x1, hst = _ssd_prompt(xc, dt, sz, ag, gb, xp, mod_p, lw)
        tm = 256
        per = L // tm
        p_mods = (mod_p, lambda k: pl.BlockSpec((None, 1, D_MODEL), lambda i, k=k: (i // per, 0, k)))
        xp = _ffn(x1.reshape(bsz * L, D_MODEL), p_mods, lw, fnw, final, tm).reshape(bsz, L, D_MODEL)
        pool_p.append(pst)
        conv_p.append(cst)
        ssm_p.append(hst.reshape(bsz, HEADS, HEADDIM, D_STATE))

        spool_t = jnp.swapaxes(state_pool[l], 0, 1)
        sconv_t = jnp.swapaxes(state_conv[l], 0, 1)
        ag, gb, sz, xc, dt, u_new, xbc_new = _mix_in_sample(xs, mod_s, lw, spool_t, sconv_t)
        hnew, y = _ssm_step(state_ssm[l].reshape(nsmp, HEADS * HEADDIM, D_STATE), xc, dt, lw)
        x1s = _mix_out_sample(y, sz, ag, gb, xs, mod_s, lw)
        s_mods = (mod_s, lambda k: pl.BlockSpec((nsmp, D_MODEL), lambda i, k=k: (0, k)))
        xs = _ffn(x1s, s_mods, lw, fnw, final, nsmp)
        pool_s.append(jnp.concatenate([state_pool[l][:, 1:], u_new[:, None, :]], axis=1))
        conv_s.append(jnp.concatenate([state_conv[l][:, 1:], xbc_new[:, None, :]], axis=1))
        ssm_s.append(hnew.reshape(nsmp, HEADS, HEADDIM, D_STATE))

    return (xp, xs.reshape(nsmp, 1, D_MODEL), jnp.stack(pool_p), jnp.stack(conv_p), jnp.stack(ssm_p),
            jnp.stack(pool_s), jnp.stack(conv_s), jnp.stack(ssm_s))
```

```python
import functools

import jax
import jax.numpy as jnp
from jax import lax
from jax.experimental import pallas as pl
from jax.experimental.pallas import tpu as pltpu

D_MODEL = 1024
POOL_WINDOWS = (2, 4, 8, 16)
POOL_GW = D_MODEL // len(POOL_WINDOWS)
POOL_BUF = max(POOL_WINDOWS) - 1
D_INNER = 2 * D_MODEL
HEADDIM = 64
HEADS = D_INNER // HEADDIM
GROUPS = 8
HPG = HEADS // GROUPS
D_STATE = 128
CONV_W = 4
CONV_DIM = D_INNER + 2 * GROUPS * D_STATE
CHUNK = 128
D_FF = ((8 * D_MODEL // 3 + 255) // 256) * 256
N_MOD = 6
COL_Z = D_MODEL
COL_XBC = COL_Z + D_INNER
COL_DT = COL_XBC + CONV_DIM
COL_GATE = COL_DT + HEADS
PAST_LEN = 16384
EPS = 1e-6

LANES = 128
MXU_N = 256
WPAD = LANES
GROUP_W = HPG * HEADDIM
POOL_HALO = 16
CONV_HALO = 8
CONV_ROWS = 64
VMEM_LIMIT = 60 * 1024 * 1024

BF16 = jnp.bfloat16
F32 = jnp.float32


def _dot(a, b):
    return jnp.dot(a, b, preferred_element_type=F32)


def _dot_nt(a, b):
    return lax.dot_general(a, b, (((1,), (1,)), ((), ())), preferred_element_type=F32)


def _dot_tn(a, b):
    return lax.dot_general(a, b, (((0,), (0,)), ((), ())), preferred_element_type=F32)


def _silu_of_half(h):
    return h * jnp.tanh(h) + h


def _silu(x):
    return _silu_of_half(0.5 * x)


def _rms(x, w):
    return x * lax.rsqrt(jnp.mean(x * x, axis=-1, keepdims=True) + EPS) * w


def _softplus(x):
    return jnp.maximum(x, 0.0) + jnp.log1p(jnp.exp(-jnp.abs(x)))


def _resident(shape):
    nd = len(shape)
    return pl.BlockSpec(shape, lambda *_: (0,) * nd, pipeline_mode=pl.Buffered(1))


def _weight(shape):
    return _resident((shape[0], shape[1] + WPAD))


def _pad_cols(w):
    return jnp.pad(w, ((0, 0), (0, WPAD))).astype(BF16)


def _mod_kernel(c_ref, w_ref, b_ref, o_ref):
    a = _silu(c_ref[...]).astype(BF16)
    o_ref[...] = _dot(a, w_ref[...].astype(BF16)) + b_ref[...]


def _modulation(c, w_ada, b_ada):
    m = c.shape[0]
    n = w_ada.shape[1]
    tn = 512
    return pl.pallas_call(
        _mod_kernel,
        grid=(n // tn,),
        in_specs=[
            pl.BlockSpec((m, D_MODEL), lambda j: (0, 0)),
            pl.BlockSpec((D_MODEL, tn), lambda j: (0, j)),
            pl.BlockSpec((1, tn), lambda j: (0, j)),
        ],
        out_specs=pl.BlockSpec((m, tn), lambda j: (0, j)),
        out_shape=jax.ShapeDtypeStruct((m, n), F32),
        name="adaln_mod",
    )(c, w_ada, b_ada.reshape(1, n))


def _norm_mod(x, sh, sc, w):
    return (_rms(x, w) * (1.0 + sc) + sh).astype(BF16)


def _pool_group_out(pooled, g, wpool_ref, pscale_ref):
    sl = slice(g * POOL_GW, (g + 1) * POOL_GW)
    return _dot(pooled.astype(BF16), wpool_ref[g]) * pscale_ref[:, sl]


def _mix_in_prompt_kernel(x_ref, sh_ref, sc_ref, n1w_ref, wu_ref, wzh_ref, wx_ref, wg_ref, wdt_ref,
                          dtb_ref, wpool_ref, pscale_ref, cw_ref, cb_ref,
                          ag_ref, gb_ref, sz_ref, xc_ref, dt_ref, pst_ref, cst_ref,
                          ubuf, xbuf, *, tm):
    i = pl.program_id(1)

    @pl.when(i == 0)
    def _():
        ubuf[0:POOL_HALO, :] = jnp.zeros((POOL_HALO, D_MODEL), F32)
        xbuf[0:CONV_HALO, :] = jnp.zeros((CONV_HALO, CONV_DIM), F32)

    @pl.when(i > 0)
    def _():
        ubuf[0:POOL_HALO, :] = ubuf[tm:tm + POOL_HALO, :]
        xbuf[0:CONV_HALO, :] = xbuf[tm:tm + CONV_HALO, :]

    hb = _norm_mod(x_ref[...], sh_ref[...], sc_ref[...], n1w_ref[...])
    dt_ref[...] = _softplus(_dot(hb, wdt_ref[...]) + dtb_ref[...])

    pos = i * tm + lax.broadcasted_iota(jnp.int32, (tm, POOL_GW), 0)

    def pool_group(g):
        w = POOL_WINDOWS[g]
        sl = slice(g * POOL_GW, (g + 1) * POOL_GW)
        ubuf[POOL_HALO:POOL_HALO + tm, sl] = _dot(hb, wu_ref[:, sl])
        full = ubuf[:, sl]
        acc = full
        span = 1
        while span < w:
            acc = acc + pltpu.roll(acc, span, axis=0)
            span *= 2
        cur = full[POOL_HALO:, :]
        cnt = jnp.minimum(pos + 1, w).astype(F32)
        a = _pool_group_out(acc[POOL_HALO:, :] / cnt - cur, g, wpool_ref, pscale_ref)
        ag_ref[:, sl] = jax.nn.sigmoid(_dot(hb, wg_ref[:, sl])) * a

    cchunk = 1024

    def gate_b_chunk(c0):
        gb_ref[:, c0:c0 + cchunk] = jax.nn.sigmoid(
            _dot(hb, wg_ref[:, D_MODEL + c0:D_MODEL + c0 + cchunk]))

    def z_chunk(c0):
        sz_ref[:, c0:c0 + cchunk] = _silu_of_half(_dot(hb, wzh_ref[:, c0:c0 + cchunk]))

    def conv_chunk(c0):
        for n0 in range(c0, c0 + cchunk, MXU_N):
            xbuf[CONV_HALO:CONV_HALO + tm, n0:n0 + MXU_N] = _dot(hb, wx_ref[:, n0:n0 + MXU_N])
        for l0 in range(c0, c0 + cchunk, LANES):
            sl = slice(l0, l0 + LANES)
            taps = 0.5 * cw_ref[:, sl]
            bias = 0.5 * cb_ref[:, sl]
            for r0 in range(0, tm, CONV_ROWS):
                src = xbuf[r0:r0 + CONV_HALO + CONV_ROWS, sl]
                acc = bias + src[CONV_HALO:, :] * taps[CONV_W - 1:CONV_W, :]
                for k in range(CONV_W - 1):
                    back = CONV_W - 1 - k
                    acc = acc + pltpu.roll(src, back, axis=0)[CONV_HALO:, :] * taps[k:k + 1, :]
                xc_ref[r0:r0 + CONV_ROWS, sl] = _silu_of_half(acc)

    light = ([functools.partial(pool_group, g) for g in range(len(POOL_WINDOWS))]
             + [functools.partial(gate_b_chunk, c0) for c0 in range(0, D_MODEL, cchunk)]
             + [functools.partial(z_chunk, c0) for c0 in range(0, D_INNER, cchunk)])
    heavy = [functools.partial(conv_chunk, c0) for c0 in range(0, CONV_DIM, cchunk)]
    for k in range(max(len(light), len(heavy))):
        if k < len(light):
            light[k]()
        if k < len(heavy):
            heavy[k]()

    @pl.when(i == pl.num_programs(1) - 1)
    def _():
        pst_ref[...] = ubuf[POOL_HALO + tm - POOL_BUF:POOL_HALO + tm, :]
        cst_ref[...] = xbuf[CONV_HALO + tm - (CONV_W - 1):CONV_HALO + tm, :]


def _mix_in_prompt(x, mod3, lw, tm=256):
    bsz, L, _ = x.shape
    nt = L // tm
    tok = lambda w: pl.BlockSpec((None, tm, w), lambda b, i: (b, i, 0))
    modspec = lambda k: pl.BlockSpec((None, 1, D_MODEL), lambda b, i, k=k: (b, 0, k))
    outs = pl.pallas_call(
        functools.partial(_mix_in_prompt_kernel, tm=tm),
        grid=(bsz, nt),
        in_specs=[
            tok(D_MODEL), modspec(0), modspec(1),
            _resident((1, D_MODEL)),
            _weight((D_MODEL, D_MODEL)), _weight((D_MODEL, D_INNER)),
            _weight((D_MODEL, CONV_DIM)), _weight((D_MODEL, 2 * D_MODEL)),
            _resident((D_MODEL, LANES)), _resident((1, LANES)),
            _resident((len(POOL_WINDOWS), POOL_GW, POOL_GW)), _resident((1, D_MODEL)),
            _resident((CONV_W, CONV_DIM)), _resident((1, CONV_DIM)),
        ],
        out_specs=[
            tok(D_MODEL), tok(D_MODEL), tok(D_INNER), tok(CONV_DIM), tok(LANES),
            pl.BlockSpec((None, POOL_BUF, D_MODEL), lambda b, i: (b, 0, 0)),
            pl.BlockSpec((None, CONV_W - 1, CONV_DIM), lambda b, i: (b, 0, 0)),
        ],
        out_shape=[
            jax.ShapeDtypeStruct((bsz, L, D_MODEL), F32),
            jax.ShapeDtypeStruct((bsz, L, D_MODEL), F32),
            jax.ShapeDtypeStruct((bsz, L, D_INNER), F32),
            jax.ShapeDtypeStruct((bsz, L, CONV_DIM), F32),
            jax.ShapeDtypeStruct((bsz, L, LANES), F32),
            jax.ShapeDtypeStruct((bsz, POOL_BUF, D_MODEL), F32),
            jax.ShapeDtypeStruct((bsz, CONV_W - 1, CONV_DIM), F32),
        ],
        scratch_shapes=[
            pltpu.VMEM((POOL_HALO + tm, D_MODEL), F32),
            pltpu.VMEM((CONV_HALO + tm, CONV_DIM), F32),
        ],
        compiler_params=pltpu.CompilerParams(
            dimension_semantics=("arbitrary", "arbitrary"), vmem_limit_bytes=VMEM_LIMIT),
        name="mix_in_prompt",
    )(x, mod3, mod3, lw["n1w"], lw["wu"], lw["wzh"], lw["wx"], lw["wg"], lw["wdt"], lw["dtb"],
      lw["wpool"], lw["pscale"], lw["cw"], lw["cb"])
    return outs


def _mixer_tail(y, sz, ag, gb, x, g1, nw, wproj_ref, wout_ref):
    yn = _rms(y * sz, nw).astype(BF16)
    b_out = _dot(yn, wproj_ref[:, 0:D_MODEL])
    mix = (ag + gb * b_out).astype(BF16)
    return x + g1 * _dot(mix, wout_ref[:, 0:D_MODEL])


def _head_pair_expand(v, h0):
    rows = v.shape[0]
    lane = lax.broadcasted_iota(jnp.int32, (rows, LANES), 1)
    lo = jnp.broadcast_to(v[:, h0:h0 + 1], (rows, LANES))
    hi = jnp.broadcast_to(v[:, h0 + 1:h0 + 2], (rows, LANES))
    return jnp.where(lane < HEADDIM, lo, hi)


def _group_expand(v, g):
    return jnp.concatenate(
        [_head_pair_expand(v, HPG * g + 2 * t) for t in range(GROUP_W // LANES)], axis=1)


def _ssd_prompt_kernel(xc_ref, dt_ref, sz_ref, ag_ref, gb_ref, x_ref, g1_ref, alog_ref, dexp_ref,
                       hexp_ref, nw_ref, wproj_ref, wout_ref,
                       x1_ref, hout_ref, ht_scr, y_scr, *, rows):
    step = pl.program_id(1)

    @pl.when(step == 0)
    def _():
        ht_scr[...] = jnp.zeros_like(ht_scr)

    q = CHUNK
    row = lax.broadcasted_iota(jnp.int32, (q, q), 0)
    lane = lax.broadcasted_iota(jnp.int32, (q, q), 1)
    causal = row >= lane
    tri = causal.astype(F32)
    a_neg = -jnp.exp(alog_ref[...])
    lane_g = lax.broadcasted_iota(jnp.int32, (q, GROUP_W), 1)

    for r0 in range(0, rows, q):
        rs = slice(r0, r0 + q)
        dt = dt_ref[rs, :]
        acs = lax.dot_general(tri, dt * a_neg, (((1,), (0,)), ((), ())),
                              precision=lax.Precision.HIGHEST, preferred_element_type=F32)
        acs_t = acs.T
        dt_t = dt.T
        w_t = dt_t * jnp.exp(acs_t[:, q - 1:q] - acs_t)

        e_acs = jnp.exp(acs)
        e_hi = e_acs.astype(BF16).astype(F32)
        e_r1 = e_acs - e_hi
        e_mid = e_r1.astype(BF16).astype(F32)
        e_lo = e_r1 - e_mid
        pieces = jnp.where(lane < HEADS, e_hi,
                           jnp.where(lane < 2 * HEADS, e_mid,
                                     jnp.where(lane < 3 * HEADS, e_lo, 0.0))).astype(BF16)
        e_exp = _dot(pieces, hexp_ref[...])

        c_bf, bt_f, cbs = [], [], []
        for g in range(GROUPS):
            b0 = D_INNER + g * D_STATE
            c0 = D_INNER + GROUPS * D_STATE + g * D_STATE
            bt = xc_ref[rs, b0:b0 + D_STATE].T
            cg = xc_ref[rs, c0:c0 + D_STATE].astype(BF16)
            c_bf.append(cg)
            bt_f.append(bt)
            cbs.append(_dot(cg, bt.astype(BF16)))

        for g in range(GROUPS):
            gs = slice(g * GROUP_W, (g + 1) * GROUP_W)
            x_g = xc_ref[rs, gs]
            scores, btw, blocks = [], [], []
            for r in range(HPG):
                h = HPG * g + r
                seg = acs[:, h:h + 1] - acs_t[h:h + 1, :]
                decay = jnp.exp(jnp.where(causal, seg, -jnp.inf))
                scores.append((cbs[g] * decay * dt_t[h:h + 1, :]).astype(BF16))
                btw.append((bt_f[g] * w_t[h:h + 1, :]).astype(BF16))
                in_head = (lane_g >= r * HEADDIM) & (lane_g < (r + 1) * HEADDIM)
                blocks.append(jnp.where(in_head, x_g, 0.0).astype(BF16))
            lhs = jnp.concatenate(
                [jnp.concatenate(scores, axis=1), jnp.concatenate(btw, axis=1)], axis=0)
            both = _dot(lhs, jnp.concatenate(blocks, axis=0))
            ht_prev = ht_scr[g * D_STATE:(g + 1) * D_STATE, :]
            y_off = _dot(c_bf[g], ht_prev.astype(BF16)) * e_exp[:, gs]
            y_scr[rs, gs] = both[0:q, :] + y_off + dexp_ref[:, gs] * x_g
            ht_scr[g * D_STATE:(g + 1) * D_STATE, :] = (
                ht_prev * e_exp[q - 1:q, gs] + both[q:2 * q, :])

    x1_ref[...] = _mixer_tail(y_scr[...], sz_ref[...], ag_ref[...], gb_ref[...], x_ref[...],
                              g1_ref[...], nw_ref[...], wproj_ref, wout_ref)

    @pl.when(step == pl.num_programs(1) - 1)
    def _():
        for g in range(GROUPS):
            hout_ref[g * GROUP_W:(g + 1) * GROUP_W, :] = ht_scr[g * D_STATE:(g + 1) * D_STATE, :].T


def _ssd_prompt(xc, dt, sz, ag, gb, x, mod3, lw, rows=4 * CHUNK):
    bsz, L, _ = x.shape
    tok = lambda w: pl.BlockSpec((None, rows, w), lambda b, c: (b, c, 0))
    return pl.pallas_call(
        functools.partial(_ssd_prompt_kernel, rows=rows),
        grid=(bsz, L // rows),
        in_specs=[
            tok(CONV_DIM), tok(LANES), tok(D_INNER), tok(D_MODEL), tok(D_MODEL), tok(D_MODEL),
            pl.BlockSpec((None, 1, D_MODEL), lambda b, c: (b, 0, 2)),
            _resident((1, LANES)), _resident((1, D_INNER)), _resident((LANES, D_INNER)),
            _resident((1, D_INNER)),
            _weight((D_INNER, D_MODEL)), _weight((D_MODEL, D_MODEL)),
        ],
        out_specs=[
            tok(D_MODEL),
            pl.BlockSpec((None, HEADS * HEADDIM, D_STATE), lambda b, c: (b, 0, 0)),
        ],
        out_shape=[
            jax.ShapeDtypeStruct((bsz, L, D_MODEL), F32),
            jax.ShapeDtypeStruct((bsz, HEADS * HEADDIM, D_STATE), F32),
        ],
        scratch_shapes=[
            pltpu.VMEM((GROUPS * D_STATE, GROUP_W), F32),
            pltpu.VMEM((rows, D_INNER), F32),
        ],
        compiler_params=pltpu.CompilerParams(
            dimension_semantics=("arbitrary", "arbitrary"), vmem_limit_bytes=VMEM_LIMIT),
        name="ssd_prompt",
    )(xc, dt, sz, ag, gb, x, mod3, lw["alog"], lw["dexp"], lw["hexp"], lw["ssd_nw"], lw["wproj"],
      lw["wout"])


def _ffn_kernel(x_ref, sh_ref, sc_ref, g2_ref, n2w_ref, win_ref, wout_ref, fnw_ref, o_ref, act_scr,
                *, final):
    x = x_ref[...]
    hb = _norm_mod(x, sh_ref[...], sc_ref[...], n2w_ref[...])
    cw = 256
    for c0 in range(0, D_FF, cw):
        gt = _dot(hb, win_ref[:, c0:c0 + cw])
        up = _dot(hb, win_ref[:, D_FF + c0:D_FF + c0 + cw])
        act_scr[:, c0:c0 + cw] = (_silu(gt) * up).astype(BF16)
    x2 = x + g2_ref[...] * _dot(act_scr[...], wout_ref[:, 0:D_MODEL])
    o_ref[...] = _rms(x2, fnw_ref[...]) if final else x2


def _ffn(x2d, mods, lw, fnw, final, tm):
    t = x2d.shape[0]
    mod_arr, modspec = mods
    return pl.pallas_call(
        functools.partial(_ffn_kernel, final=final),
        grid=(t // tm,),
        in_specs=[
            pl.BlockSpec((tm, D_MODEL), lambda i: (i, 0)),
            modspec(3), modspec(4), modspec(5),
            _resident((1, D_MODEL)),
            _resident((D_MODEL, 2 * D_FF)), _weight((D_FF, D_MODEL)),
            _resident((1, D_MODEL)),
        ],
        out_specs=pl.BlockSpec((tm, D_MODEL), lambda i: (i, 0)),
        out_shape=jax.ShapeDtypeStruct((t, D_MODEL), F32),
        scratch_shapes=[pltpu.VMEM((tm, D_FF), BF16)],
        compiler_params=pltpu.CompilerParams(
            dimension_semantics=("arbitrary",), vmem_limit_bytes=VMEM_LIMIT),
        name="ffn",
    )(x2d, mod_arr, mod_arr, mod_arr, lw["n2w"], lw["wffn_in"], lw["wffn_out"], fnw)


def _mix_in_sample_kernel(x_ref, sh_ref, sc_ref, n1w_ref, wu_ref, wzh_ref, wx_ref, wg_ref, wdt_ref,
                          dtb_ref, wpool_ref, pscale_ref, cw_ref, cb_ref, spool_ref, sconv_ref,
                          ag_ref, gb_ref, sz_ref, xc_ref, dt_ref, u_ref, xraw_ref):
    hb = _norm_mod(x_ref[...], sh_ref[...], sc_ref[...], n1w_ref[...])
    u = _dot(hb, wu_ref[:, 0:D_MODEL])
    u_ref[...] = u
    sz_ref[...] = _silu_of_half(_dot(hb, wzh_ref[:, 0:D_INNER]))
    xbc = _dot(hb, wx_ref[:, 0:CONV_DIM])
    xraw_ref[...] = xbc
    gates = jax.nn.sigmoid(_dot(hb, wg_ref[:, 0:2 * D_MODEL]))
    gb_ref[...] = gates[:, D_MODEL:]
    dt_ref[...] = _softplus(_dot(hb, wdt_ref[...]) + dtb_ref[...])

    for g, w in enumerate(POOL_WINDOWS):
        sl = slice(g * POOL_GW, (g + 1) * POOL_GW)
        cur = u[:, sl]
        acc = cur
        for k in range(1, w):
            acc = acc + spool_ref[:, POOL_BUF - k, sl]
        cnt = float(min(PAST_LEN + 1, w))
        a = _pool_group_out(acc / cnt - cur, g, wpool_ref, pscale_ref)
        ag_ref[:, sl] = gates[:, sl] * a

    acc = cb_ref[...] + xbc * cw_ref[CONV_W - 1:CONV_W, :]
    for k in range(CONV_W - 1):
        acc = acc + sconv_ref[:, k, :] * cw_ref[k:k + 1, :]
    xc_ref[...] = _silu(acc)


def _mix_in_sample(x2d, mod_s, lw, spool_t, sconv_t):
    n = x2d.shape[0]
    full = lambda r, c: pl.BlockSpec((r, c), lambda i: (0, 0), pipeline_mode=pl.Buffered(1))
    full3 = lambda a, r, c: pl.BlockSpec((a, r, c), lambda i: (0, 0, 0), pipeline_mode=pl.Buffered(1))
    modspec = lambda k: pl.BlockSpec((n, D_MODEL), lambda i, k=k: (0, k),
                                     pipeline_mode=pl.Buffered(1))
    return pl.pallas_call(
        _mix_in_sample_kernel,
        grid=(1,),
        in_specs=[
            full(n, D_MODEL), modspec(0), modspec(1),
            _resident((1, D_MODEL)),
            _weight((D_MODEL, D_MODEL)), _weight((D_MODEL, D_INNER)),
            _weight((D_MODEL, CONV_DIM)), _weight((D_MODEL, 2 * D_MODEL)),
            _resident((D_MODEL, LANES)), _resident((1, LANES)),
            _resident((len(POOL_WINDOWS), POOL_GW, POOL_GW)), _resident((1, D_MODEL)),
            _resident((CONV_W, CONV_DIM)), _resident((1, CONV_DIM)),
            full3(n, POOL_BUF, D_MODEL), full3(n, CONV_W - 1, CONV_DIM),
        ],
        out_specs=[
            full(n, D_MODEL), full(n, D_MODEL), full(n, D_INNER), full(n, CONV_DIM), full(n, LANES),
            full(n, D_MODEL), full(n, CONV_DIM),
        ],
        out_shape=[
            jax.ShapeDtypeStruct((n, D_MODEL), F32),
            jax.ShapeDtypeStruct((n, D_MODEL), F32),
            jax.ShapeDtypeStruct((n, D_INNER), F32),
            jax.ShapeDtypeStruct((n, CONV_DIM), F32),
            jax.ShapeDtypeStruct((n, LANES), F32),
            jax.ShapeDtypeStruct((n, D_MODEL), F32),
            jax.ShapeDtypeStruct((n, CONV_DIM), F32),
        ],
        compiler_params=pltpu.CompilerParams(
            dimension_semantics=("arbitrary",), vmem_limit_bytes=VMEM_LIMIT),
        name="mix_in_sample",
    )(x2d, mod_s, mod_s, lw["n1w"], lw["wu"], lw["wzh"], lw["wx"], lw["wg"], lw["wdt"], lw["dtb"],
      lw["wpool"], lw["pscale"], lw["cw"], lw["cb"], spool_t, sconv_t)


def _ssm_step_kernel(h_ref, xc_ref, dt_ref, alog_ref, dexp_ref, hout_ref, y_ref, *, bb):
    lane = lax.broadcasted_iota(jnp.int32, (bb, LANES), 1)
    head_ok = lane < HEADS
    dt = jnp.where(head_ok, dt_ref[...], 0.0)
    a_neg = jnp.where(head_ok[0:1, :], -jnp.exp(alog_ref[...]), 0.0)
    dec = jnp.exp(dt * a_neg)
    xs = xc_ref[:, 0:D_INNER]
    dt_exp = jnp.concatenate([_group_expand(dt, g) for g in range(GROUPS)], axis=1)
    pad = jnp.zeros((bb, D_INNER), F32)
    xdt = jnp.concatenate([xs * dt_exp, pad], axis=0).astype(BF16)
    bm = xc_ref[:, D_INNER:D_INNER + GROUPS * D_STATE]
    cm = xc_ref[:, D_INNER + GROUPS * D_STATE:]
    rowid = lax.broadcasted_iota(jnp.int32, (2 * bb, D_STATE), 0)
    rowid_y = lax.broadcasted_iota(jnp.int32, (2 * bb, GROUP_W), 0)
    padn = jnp.zeros((bb, D_STATE), F32)

    y_acc = [jnp.zeros((2 * bb, GROUP_W), F32) for _ in range(GROUPS)]
    for j in range(bb):
        for g in range(GROUPS):
            rs = slice(g * GROUP_W, (g + 1) * GROUP_W)
            ns = slice(g * D_STATE, (g + 1) * D_STATE)
            h_g = h_ref[j, rs, :]
            dcol = jnp.concatenate(
                [jnp.broadcast_to(dec[j:j + 1, HPG * g + r:HPG * g + r + 1], (HEADDIM, D_STATE))
                 for r in range(HPG)], axis=0)
            b16 = jnp.concatenate([bm[:, ns], padn], axis=0)
            b_j = jnp.where(rowid == j, b16, 0.0).astype(BF16)
            new = h_g * dcol + _dot_tn(xdt[:, rs], b_j)
            hout_ref[j, rs, :] = new
            c16 = jnp.concatenate([cm[:, ns], padn], axis=0).astype(BF16)
            y_all = _dot_nt(c16, new.astype(BF16))
            y_acc[g] = y_acc[g] + jnp.where(rowid_y == j, y_all, 0.0)
    y = jnp.concatenate(y_acc, axis=1)[0:bb, :]
    y_ref[...] = y + dexp_ref[...] * xs


def _ssm_step(h3, xc, dt, lw, bb=8):
    n = h3.shape[0]
    return pl.pallas_call(
        functools.partial(_ssm_step_kernel, bb=bb),
        grid=(n // bb,),
        in_specs=[
            pl.BlockSpec((bb, HEADS * HEADDIM, D_STATE), lambda i: (i, 0, 0)),
            pl.BlockSpec((bb, CONV_DIM), lambda i: (i, 0)),
            pl.BlockSpec((bb, LANES), lambda i: (i, 0)),
            _resident((1, LANES)), _resident((1, D_INNER)),
        ],
        out_specs=[
            pl.BlockSpec((bb, HEADS * HEADDIM, D_STATE), lambda i: (i, 0, 0)),
            pl.BlockSpec((bb, D_INNER), lambda i: (i, 0)),
        ],
        out_shape=[
            jax.ShapeDtypeStruct(h3.shape, F32),
            jax.ShapeDtypeStruct((n, D_INNER), F32),
        ],
        compiler_params=pltpu.CompilerParams(
            dimension_semantics=("arbitrary",), vmem_limit_bytes=VMEM_LIMIT),
        name="ssm_step",
    )(h3, xc, dt, lw["alog"], lw["dexp"])


def _mix_out_sample_kernel(y_ref, sz_ref, ag_ref, gb_ref, x_ref, g1_ref, nw_ref, wproj_ref, wout_ref,
                           o_ref):
    o_ref[...] = _mixer_tail(y_ref[...], sz_ref[...], ag_ref[...], gb_ref[...], x_ref[...],
                             g1_ref[...], nw_ref[...], wproj_ref, wout_ref)


def _mix_out_sample(y, sz, ag, gb, x2d, mod_s, lw):
    n = x2d.shape[0]
    full = lambda c: pl.BlockSpec((n, c), lambda i: (0, 0))
    return pl.pallas_call(
        _mix_out_sample_kernel,
        grid=(1,),
        in_specs=[
            full(D_INNER), full(D_INNER), full(D_MODEL), full(D_MODEL), full(D_MODEL),
            pl.BlockSpec((n, D_MODEL), lambda i: (0, 2)),
            _resident((1, D_INNER)), _weight((D_INNER, D_MODEL)), _weight((D_MODEL, D_MODEL)),
        ],
        out_specs=full(D_MODEL),
        out_shape=jax.ShapeDtypeStruct((n, D_MODEL), F32),
        compiler_params=pltpu.CompilerParams(
            dimension_semantics=("arbitrary",), vmem_limit_bytes=VMEM_LIMIT),
        name="mix_out_sample",
    )(y, sz, ag, gb, x2d, mod_s, lw["ssd_nw"], lw["wproj"], lw["wout"])


def _layer_weights(l, w_in, w_pool, pool_scale, conv_w, conv_b, dt_bias, A_log, D_skip, ssd_norm_w,
                   w_ssd_proj, w_out, norm1_w, norm2_w, w_ffn_in, w_ffn_out):
    wi = w_in[l]
    rep = LANES // HEADS
    lane_id = jnp.arange(LANES)[:, None]
    col_head = jnp.arange(D_INNER)[None, :] // HEADDIM
    hexp = ((lane_id % HEADS == col_head) & (lane_id < 3 * HEADS)).astype(BF16)
    return dict(
        n1w=norm1_w[l].reshape(1, D_MODEL),
        wu=_pad_cols(wi[:, :COL_Z]),
        wzh=_pad_cols(0.5 * wi[:, COL_Z:COL_XBC]),
        wx=_pad_cols(wi[:, COL_XBC:COL_DT]),
        wdt=jnp.tile(wi[:, COL_DT:COL_GATE], (1, rep)).astype(BF16),
        wg=_pad_cols(wi[:, COL_GATE:]),
        dtb=jnp.tile(dt_bias[l], rep).reshape(1, LANES),
        wpool=w_pool[l].astype(BF16),
        pscale=pool_scale[l].reshape(1, D_MODEL),
        cw=conv_w[l],
        cb=conv_b[l].reshape(1, CONV_DIM),
        alog=jnp.tile(A_log[l], rep).reshape(1, LANES),
        hexp=hexp,
        dexp=jnp.repeat(D_skip[l], HEADDIM).reshape(1, D_INNER),
        ssd_nw=ssd_norm_w[l].reshape(1, D_INNER),
        wproj=_pad_cols(w_ssd_proj[l]),
        wout=_pad_cols(w_out[l]),
        n2w=norm2_w[l].reshape(1, D_MODEL),
        wffn_in=w_ffn_in[l].astype(BF16),
        wffn_out=_pad_cols(w_ffn_out[l]),
    )


def kernel(x_prompt, x_sample, c_prompt, c_sample, state_pool, state_conv, state_ssm, w_ada, b_ada, norm1_w, w_in, w_pool, pool_scale, conv_w, conv_b, dt_bias, A_log, D_skip, ssd_norm_w, w_ssd_proj, w_out, norm2_w, w_ffn_in, w_ffn_out, final_norm_w):
    depth = w_in.shape[0]
    bsz, L, _ = x_prompt.shape
    nsmp = x_sample.shape[0]
    xp = x_prompt
    xs = x_sample.reshape(nsmp, D_MODEL)
    c_all = jnp.concatenate([c_prompt, c_sample], axis=0)
    fnw = final_norm_w.reshape(1, D_MODEL)
    pool_p, conv_p, ssm_p, pool_s, conv_s, ssm_s = [], [], [], [], [], []
    for l in range(depth):
        final = l == depth - 1
        lw = _layer_weights(l, w_in, w_pool, pool_scale, conv_w, conv_b, dt_bias, A_log, D_skip,
                            ssd_norm_w, w_ssd_proj, w_out, norm1_w, norm2_w, w_ffn_in, w_ffn_out)
        mod = _modulation(c_all, w_ada[l], b_ada[l])
        mod_p = mod[:bsz].reshape(bsz, 1, N_MOD * D_MODEL)
        mod_s = mod[bsz:]

        ag, gb, sz, xc, dt, pst, cst = _mix_in_prompt(xp, mod_p, lw)
        x1, hst = _ssd_prompt(xc, dt, sz, ag, gb, xp, mod_p, lw)
        tm = 256
        per = L // tm
        p_mods = (mod_p, lambda k: pl.BlockSpec((None, 1, D_MODEL), lambda i, k=k: (i // per, 0, k)))
        xp = _ffn(x1.reshape(bsz * L, D_MODEL), p_mods, lw, fnw, final, tm).reshape(bsz, L, D_MODEL)
        pool_p.append(pst)
        conv_p.append(cst)
        ssm_p.append(hst.reshape(bsz, HEADS, HEADDIM, D_STATE))

        ag, gb, sz, xc, dt, u_new, xbc_new = _mix_in_sample(xs, mod_s, lw, state_pool[l],
                                                            state_conv[l])
        hnew, y = _ssm_step(state_ssm[l].reshape(nsmp, HEADS * HEADDIM, D_STATE), xc, dt, lw)
        x1s = _mix_out_sample(y, sz, ag, gb, xs, mod_s, lw)
        s_mods = (mod_s, lambda k: pl.BlockSpec((nsmp, D_MODEL), lambda i, k=k: (0, k)))
        xs = _ffn(x1s, s_mods, lw, fnw, final, nsmp)
        pool_s.append(jnp.concatenate([state_pool[l][:, 1:], u_new[:, None, :]], axis=1))
        conv_s.append(jnp.concatenate([state_conv[l][:, 1:], xbc_new[:, None, :]], axis=1))
        ssm_s.append(hnew.reshape(nsmp, HEADS, HEADDIM, D_STATE))

    return (xp, xs.reshape(nsmp, 1, D_MODEL), jnp.stack(pool_p), jnp.stack(conv_p), jnp.stack(ssm_p),
            jnp.stack(pool_s), jnp.stack(conv_s), jnp.stack(ssm_s))
```

```python
import functools

import jax
import jax.numpy as jnp
from jax import lax
from jax.experimental import pallas as pl
from jax.experimental.pallas import tpu as pltpu

D_MODEL = 1024
POOL_WINDOWS = (2, 4, 8, 16)
POOL_GW = D_MODEL // len(POOL_WINDOWS)
POOL_BUF = max(POOL_WINDOWS) - 1
D_INNER = 2 * D_MODEL
HEADDIM = 64
HEADS = D_INNER // HEADDIM
GROUPS = 8
HPG = HEADS // GROUPS
D_STATE = 128
CONV_W = 4
CONV_DIM = D_INNER + 2 * GROUPS * D_STATE
CHUNK = 128
D_FF = ((8 * D_MODEL // 3 + 255) // 256) * 256
N_MOD = 6
COL_Z = D_MODEL
COL_XBC = COL_Z + D_INNER
COL_DT = COL_XBC + CONV_DIM
COL_GATE = COL_DT + HEADS
IN_COLS = COL_GATE + 2 * D_MODEL
PAST_LEN = 16384
EPS = 1e-6

LANES = 128
MXU_N = 256
WPAD = LANES
GROUP_W = HPG * HEADDIM
POOL_HALO = 16
CONV_HALO = 8
CONV_ROWS = 64
VMEM_LIMIT = 60 * 1024 * 1024

BF16 = jnp.bfloat16
F32 = jnp.float32


def _dot(a, b):
    return jnp.dot(a, b, preferred_element_type=F32)


def _dot_nt(a, b):
    return lax.dot_general(a, b, (((1,), (1,)), ((), ())), preferred_element_type=F32)


def _dot_tn(a, b):
    return lax.dot_general(a, b, (((0,), (0,)), ((), ())), preferred_element_type=F32)


def _silu_of_half(h):
    return h * jnp.tanh(h) + h


def _silu(x):
    return _silu_of_half(0.5 * x)


def _rms(x, w):
    return x * lax.rsqrt(jnp.mean(x * x, axis=-1, keepdims=True) + EPS) * w


def _softplus(x):
    return jnp.maximum(x, 0.0) + jnp.log1p(jnp.exp(-jnp.abs(x)))


def _resident(shape):
    nd = len(shape)
    return pl.BlockSpec(shape, lambda *_: (0,) * nd, pipeline_mode=pl.Buffered(1))


def _weight(shape):
    return _resident((shape[0], shape[1] + WPAD))


def _cast_weight_kernel(w_ref, o_ref):
    n = w_ref.shape[1]
    o_ref[:, 0:n] = w_ref[...].astype(BF16)
    if o_ref.shape[1] > n:
        o_ref[:, n:] = jnp.zeros((o_ref.shape[0], o_ref.shape[1] - n), BF16)


def _cast_weight(w, l, pad=WPAD, rows=256):
    _, k, n = w.shape
    return pl.pallas_call(
        _cast_weight_kernel,
        grid=(k // rows,),
        in_specs=[pl.BlockSpec((None, rows, n), lambda i: (l, i, 0))],
        out_specs=pl.BlockSpec((rows, n + pad), lambda i: (i, 0)),
        out_shape=jax.ShapeDtypeStruct((k, n + pad), BF16),
        compiler_params=pltpu.CompilerParams(dimension_semantics=("arbitrary",)),
        name="cast_weight",
    )(w)


def _mod_kernel(c_ref, w_ref, b_ref, o_ref):
    a = _silu(c_ref[...]).astype(BF16)
    o_ref[...] = _dot(a, w_ref[...].astype(BF16)) + b_ref[...]


def _modulation(c, w_ada, b_ada):
    m = c.shape[0]
    n = w_ada.shape[1]
    tn = 512
    return pl.pallas_call(
        _mod_kernel,
        grid=(n // tn,),
        in_specs=[
            pl.BlockSpec((m, D_MODEL), lambda j: (0, 0)),
            pl.BlockSpec((D_MODEL, tn), lambda j: (0, j)),
            pl.BlockSpec((1, tn), lambda j: (0, j)),
        ],
        out_specs=pl.BlockSpec((m, tn), lambda j: (0, j)),
        out_shape=jax.ShapeDtypeStruct((m, n), F32),
        name="adaln_mod",
    )(c, w_ada, b_ada.reshape(1, n))


def _prep_w_in_kernel(w_ref, wu_ref, wzh_ref, wx_ref, wg_ref, wdt_ref):
    rows = w_ref.shape[0]

    def put(dst_ref, lo, hi, scale=None):
        w = w_ref[:, lo:hi]
        if scale is not None:
            w = scale * w
        dst_ref[:, 0:hi - lo] = w.astype(BF16)
        dst_ref[:, hi - lo:hi - lo + WPAD] = jnp.zeros((rows, WPAD), BF16)

    put(wu_ref, 0, COL_Z)
    put(wzh_ref, COL_Z, COL_XBC, 0.5)
    put(wx_ref, COL_XBC, COL_DT)
    put(wg_ref, COL_GATE, IN_COLS)
    d = w_ref[:, COL_DT:COL_GATE]
    wdt_ref[...] = jnp.concatenate([d] * (LANES // HEADS), axis=1).astype(BF16)


def _prep_w_in(w_in, l, rows=128):
    outs = [D_MODEL + WPAD, D_INNER + WPAD, CONV_DIM + WPAD, 2 * D_MODEL + WPAD, LANES]
    wu, wzh, wx, wg, wdt = pl.pallas_call(
        _prep_w_in_kernel,
        grid=(D_MODEL // rows,),
        in_specs=[pl.BlockSpec((None, rows, IN_COLS), lambda i: (l, i, 0))],
        out_specs=[pl.BlockSpec((rows, n), lambda i: (i, 0)) for n in outs],
        out_shape=[jax.ShapeDtypeStruct((D_MODEL, n), BF16) for n in outs],
        compiler_params=pltpu.CompilerParams(
            dimension_semantics=("arbitrary",), vmem_limit_bytes=VMEM_LIMIT),
        name="prep_w_in",
    )(w_in)
    return dict(wu=wu, wzh=wzh, wx=wx, wg=wg, wdt=wdt)


def _norm_mod(x, sh, sc, w):
    return (_rms(x, w) * (1.0 + sc) + sh).astype(BF16)


def _pool_group_out(pooled, g, wpool_ref, pscale_ref):
    sl = slice(g * POOL_GW, (g + 1) * POOL_GW)
    return _dot(pooled.astype(BF16), wpool_ref[g]) * pscale_ref[:, sl]


def _mix_in_prompt_kernel(x_ref, sh_ref, sc_ref, n1w_ref, wu_ref, wzh_ref, wx_ref, wg_ref, wdt_ref,
                          dtb_ref, wpool_ref, pscale_ref, cw_ref, cb_ref,
                          ag_ref, gb_ref, sz_ref, xc_ref, dt_ref, pst_ref, cst_ref,
                          ubuf, xbuf, *, tm):
    i = pl.program_id(1)

    @pl.when(i == 0)
    def _():
        ubuf[0:POOL_HALO, :] = jnp.zeros((POOL_HALO, D_MODEL), F32)
        xbuf[0:CONV_HALO, :] = jnp.zeros((CONV_HALO, CONV_DIM), F32)

    @pl.when(i > 0)
    def _():
        ubuf[0:POOL_HALO, :] = ubuf[tm:tm + POOL_HALO, :]
        xbuf[0:CONV_HALO, :] = xbuf[tm:tm + CONV_HALO, :]

    hb = _norm_mod(x_ref[...], sh_ref[...], sc_ref[...], n1w_ref[...])
    dt_ref[...] = _softplus(_dot(hb, wdt_ref[...]) + dtb_ref[...])

    pos = i * tm + lax.broadcasted_iota(jnp.int32, (tm, POOL_GW), 0)

    def pool_group(g):
        w = POOL_WINDOWS[g]
        sl = slice(g * POOL_GW, (g + 1) * POOL_GW)
        ubuf[POOL_HALO:POOL_HALO + tm, sl] = _dot(hb, wu_ref[:, sl])
        full = ubuf[:, sl]
        acc = full
        span = 1
        while span < w:
            acc = acc + pltpu.roll(acc, span, axis=0)
            span *= 2
        cur = full[POOL_HALO:, :]
        cnt = jnp.minimum(pos + 1, w).astype(F32)
        a = _pool_group_out(acc[POOL_HALO:, :] / cnt - cur, g, wpool_ref, pscale_ref)
        ag_ref[:, sl] = jax.nn.sigmoid(_dot(hb, wg_ref[:, sl])) * a

    cchunk = 1024

    def gate_b_chunk(c0):
        gb_ref[:, c0:c0 + cchunk] = jax.nn.sigmoid(
            _dot(hb, wg_ref[:, D_MODEL + c0:D_MODEL + c0 + cchunk]))

    def z_chunk(c0):
        sz_ref[:, c0:c0 + cchunk] = _silu_of_half(_dot(hb, wzh_ref[:, c0:c0 + cchunk]))

    def conv_chunk(c0):
        for n0 in range(c0, c0 + cchunk, MXU_N):
            xbuf[CONV_HALO:CONV_HALO + tm, n0:n0 + MXU_N] = _dot(hb, wx_ref[:, n0:n0 + MXU_N])
        for l0 in range(c0, c0 + cchunk, LANES):
            sl = slice(l0, l0 + LANES)
            taps = 0.5 * cw_ref[:, sl]
            bias = 0.5 * cb_ref[:, sl]
            for r0 in range(0, tm, CONV_ROWS):
                src = xbuf[r0:r0 + CONV_HALO + CONV_ROWS, sl]
                acc = bias + src[CONV_HALO:, :] * taps[CONV_W - 1:CONV_W, :]
                for k in range(CONV_W - 1):
                    back = CONV_W - 1 - k
                    acc = acc + pltpu.roll(src, back, axis=0)[CONV_HALO:, :] * taps[k:k + 1, :]
                xc_ref[r0:r0 + CONV_ROWS, sl] = _silu_of_half(acc)

    light = ([functools.partial(pool_group, g) for g in range(len(POOL_WINDOWS))]
             + [functools.partial(gate_b_chunk, c0) for c0 in range(0, D_MODEL, cchunk)]
             + [functools.partial(z_chunk, c0) for c0 in range(0, D_INNER, cchunk)])
    heavy = [functools.partial(conv_chunk, c0) for c0 in range(0, CONV_DIM, cchunk)]
    for k in range(max(len(light), len(heavy))):
        if k < len(light):
            light[k]()
        if k < len(heavy):
            heavy[k]()

    @pl.when(i == pl.num_programs(1) - 1)
    def _():
        pst_ref[...] = ubuf[POOL_HALO + tm - POOL_BUF:POOL_HALO + tm, :]
        cst_ref[...] = xbuf[CONV_HALO + tm - (CONV_W - 1):CONV_HALO + tm, :]


def _mix_in_prompt(x, mod3, lw, tm=256):
    bsz, L, _ = x.shape
    nt = L // tm
    tok = lambda w: pl.BlockSpec((None, tm, w), lambda b, i: (b, i, 0))
    modspec = lambda k: pl.BlockSpec((None, 1, D_MODEL), lambda b, i, k=k: (b, 0, k))
    outs = pl.pallas_call(
        functools.partial(_mix_in_prompt_kernel, tm=tm),
        grid=(bsz, nt),
        in_specs=[
            tok(D_MODEL), modspec(0), modspec(1),
            _resident((1, D_MODEL)),
            _weight((D_MODEL, D_MODEL)), _weight((D_MODEL, D_INNER)),
            _weight((D_MODEL, CONV_DIM)), _weight((D_MODEL, 2 * D_MODEL)),
            _resident((D_MODEL, LANES)), _resident((1, LANES)),
            _resident((len(POOL_WINDOWS), POOL_GW, POOL_GW)), _resident((1, D_MODEL)),
            _resident((CONV_W, CONV_DIM)), _resident((1, CONV_DIM)),
        ],
        out_specs=[
            tok(D_MODEL), tok(D_MODEL), tok(D_INNER), tok(CONV_DIM), tok(LANES),
            pl.BlockSpec((None, POOL_BUF, D_MODEL), lambda b, i: (b, 0, 0)),
            pl.BlockSpec((None, CONV_W - 1, CONV_DIM), lambda b, i: (b, 0, 0)),
        ],
        out_shape=[
            jax.ShapeDtypeStruct((bsz, L, D_MODEL), F32),
            jax.ShapeDtypeStruct((bsz, L, D_MODEL), F32),
            jax.ShapeDtypeStruct((bsz, L, D_INNER), F32),
            jax.ShapeDtypeStruct((bsz, L, CONV_DIM), F32),
            jax.ShapeDtypeStruct((bsz, L, LANES), F32),
            jax.ShapeDtypeStruct((bsz, POOL_BUF, D_MODEL), F32),
            jax.ShapeDtypeStruct((bsz, CONV_W - 1, CONV_DIM), F32),
        ],
        scratch_shapes=[
            pltpu.VMEM((POOL_HALO + tm, D_MODEL), F32),
            pltpu.VMEM((CONV_HALO + tm, CONV_DIM), F32),
        ],
        compiler_params=pltpu.CompilerParams(
            dimension_semantics=("arbitrary", "arbitrary"), vmem_limit_bytes=VMEM_LIMIT),
        name="mix_in_prompt",
    )(x, mod3, mod3, lw["n1w"], lw["wu"], lw["wzh"], lw["wx"], lw["wg"], lw["wdt"], lw["dtb"],
      lw["wpool"], lw["pscale"], lw["cw"], lw["cb"])
    return outs


def _mixer_tail(y, sz, ag, gb, x, g1, nw, wproj_ref, wout_ref):
    yn = _rms(y * sz, nw).astype(BF16)
    b_out = _dot(yn, wproj_ref[:, 0:D_MODEL])
    mix = (ag + gb * b_out).astype(BF16)
    return x + g1 * _dot(mix, wout_ref[:, 0:D_MODEL])


def _head_pair_expand(v, h0):
    rows = v.shape[0]
    lane = lax.broadcasted_iota(jnp.int32, (rows, LANES), 1)
    lo = jnp.broadcast_to(v[:, h0:h0 + 1], (rows, LANES))
    hi = jnp.broadcast_to(v[:, h0 + 1:h0 + 2], (rows, LANES))
    return jnp.where(lane < HEADDIM, lo, hi)


def _group_expand(v, g):
    return jnp.concatenate(
        [_head_pair_expand(v, HPG * g + 2 * t) for t in range(GROUP_W // LANES)], axis=1)


def _ssd_prompt_kernel(xc_ref, dt_ref, sz_ref, ag_ref, gb_ref, x_ref, g1_ref, alog_ref, dexp_ref,
                       hexp_ref, nw_ref, wproj_ref, wout_ref,
                       x1_ref, hout_ref, ht_scr, y_scr, *, rows):
    step = pl.program_id(1)

    @pl.when(step == 0)
    def _():
        ht_scr[...] = jnp.zeros_like(ht_scr)

    q = CHUNK
    row = lax.broadcasted_iota(jnp.int32, (q, q), 0)
    lane = lax.broadcasted_iota(jnp.int32, (q, q), 1)
    causal = row >= lane
    tri = causal.astype(F32)
    a_neg = -jnp.exp(alog_ref[...])
    lane_g = lax.broadcasted_iota(jnp.int32, (q, GROUP_W), 1)

    for r0 in range(0, rows, q):
        rs = slice(r0, r0 + q)
        dt = dt_ref[rs, :]
        acs = lax.dot_general(tri, dt * a_neg, (((1,), (0,)), ((), ())),
                              precision=lax.Precision.HIGHEST, preferred_element_type=F32)
        acs_t = acs.T
        dt_t = dt.T
        w_t = dt_t * jnp.exp(acs_t[:, q - 1:q] - acs_t)

        e_acs = jnp.exp(acs)
        e_hi = e_acs.astype(BF16).astype(F32)
        e_r1 = e_acs - e_hi
        e_mid = e_r1.astype(BF16).astype(F32)
        e_lo = e_r1 - e_mid
        pieces = jnp.where(lane < HEADS, e_hi,
                           jnp.where(lane < 2 * HEADS, e_mid,
                                     jnp.where(lane < 3 * HEADS, e_lo, 0.0))).astype(BF16)
        e_exp = _dot(pieces, hexp_ref[...])

        c_bf, bt_f, cbs = [], [], []
        for g in range(GROUPS):
            b0 = D_INNER + g * D_STATE
            c0 = D_INNER + GROUPS * D_STATE + g * D_STATE
            bt = xc_ref[rs, b0:b0 + D_STATE].T
            cg = xc_ref[rs, c0:c0 + D_STATE].astype(BF16)
            c_bf.append(cg)
            bt_f.append(bt)
            cbs.append(_dot(cg, bt.astype(BF16)))

        for g in range(GROUPS):
            gs = slice(g * GROUP_W, (g + 1) * GROUP_W)
            x_g = xc_ref[rs, gs]
            scores, btw, blocks = [], [], []
            for r in range(HPG):
                h = HPG * g + r
                seg = acs[:, h:h + 1] - acs_t[h:h + 1, :]
                decay = jnp.exp(jnp.where(causal, seg, -jnp.inf))
                scores.append((cbs[g] * decay * dt_t[h:h + 1, :]).astype(BF16))
                btw.append((bt_f[g] * w_t[h:h + 1, :]).astype(BF16))
                in_head = (lane_g >= r * HEADDIM) & (lane_g < (r + 1) * HEADDIM)
                blocks.append(jnp.where(in_head, x_g, 0.0).astype(BF16))
            lhs = jnp.concatenate(
                [jnp.concatenate(scores, axis=1), jnp.concatenate(btw, axis=1)], axis=0)
            both = _dot(lhs, jnp.concatenate(blocks, axis=0))
            ht_prev = ht_scr[g * D_STATE:(g + 1) * D_STATE, :]
            y_off = _dot(c_bf[g], ht_prev.astype(BF16)) * e_exp[:, gs]
            y_scr[rs, gs] = both[0:q, :] + y_off + dexp_ref[:, gs] * x_g
            ht_scr[g * D_STATE:(g + 1) * D_STATE, :] = (
                ht_prev * e_exp[q - 1:q, gs] + both[q:2 * q, :])

    x1_ref[...] = _mixer_tail(y_scr[...], sz_ref[...], ag_ref[...], gb_ref[...], x_ref[...],
                              g1_ref[...], nw_ref[...], wproj_ref, wout_ref)

    @pl.when(step == pl.num_programs(1) - 1)
    def _():
        for g in range(GROUPS):
            hout_ref[g * GROUP_W:(g + 1) * GROUP_W, :] = ht_scr[g * D_STATE:(g + 1) * D_STATE, :].T


def _ssd_prompt(xc, dt, sz, ag, gb, x, mod3, lw, rows=4 * CHUNK):
    bsz, L, _ = x.shape
    tok = lambda w: pl.BlockSpec((None, rows, w), lambda b, c: (b, c, 0))
    return pl.pallas_call(
        functools.partial(_ssd_prompt_kernel, rows=rows),
        grid=(bsz, L // rows),
        in_specs=[
            tok(CONV_DIM), tok(LANES), tok(D_INNER), tok(D_MODEL), tok(D_MODEL), tok(D_MODEL),
            pl.BlockSpec((None, 1, D_MODEL), lambda b, c: (b, 0, 2)),
            _resident((1, LANES)), _resident((1, D_INNER)), _resident((LANES, D_INNER)),
            _resident((1, D_INNER)),
            _weight((D_INNER, D_MODEL)), _weight((D_MODEL, D_MODEL)),
        ],
        out_specs=[
            tok(D_MODEL),
            pl.BlockSpec((None, HEADS * HEADDIM, D_STATE), lambda b, c: (b, 0, 0)),
        ],
        out_shape=[
            jax.ShapeDtypeStruct((bsz, L, D_MODEL), F32),
            jax.ShapeDtypeStruct((bsz, HEADS * HEADDIM, D_STATE), F32),
        ],
        scratch_shapes=[
            pltpu.VMEM((GROUPS * D_STATE, GROUP_W), F32),
            pltpu.VMEM((rows, D_INNER), F32),
        ],
        compiler_params=pltpu.CompilerParams(
            dimension_semantics=("arbitrary", "arbitrary"), vmem_limit_bytes=VMEM_LIMIT),
        name="ssd_prompt",
    )(xc, dt, sz, ag, gb, x, mod3, lw["alog"], lw["dexp"], lw["hexp"], lw["ssd_nw"], lw["wproj"],
      lw["wout"])


def _ffn_kernel(x_ref, sh_ref, sc_ref, g2_ref, n2w_ref, win_ref, wout_ref, fnw_ref, o_ref, act_scr,
                *, final):
    x = x_ref[...]
    hb = _norm_mod(x, sh_ref[...], sc_ref[...], n2w_ref[...])
    cw = 256
    for c0 in range(0, D_FF, cw):
        gt = _dot(hb, win_ref[:, c0:c0 + cw])
        up = _dot(hb, win_ref[:, D_FF + c0:D_FF + c0 + cw])
        act_scr[:, c0:c0 + cw] = (_silu(gt) * up).astype(BF16)
    x2 = x + g2_ref[...] * _dot(act_scr[...], wout_ref[:, 0:D_MODEL])
    o_ref[...] = _rms(x2, fnw_ref[...]) if final else x2


def _ffn(x2d, mods, lw, fnw, final, tm):
    t = x2d.shape[0]
    mod_arr, modspec = mods
    return pl.pallas_call(
        functools.partial(_ffn_kernel, final=final),
        grid=(t // tm,),
        in_specs=[
            pl.BlockSpec((tm, D_MODEL), lambda i: (i, 0)),
            modspec(3), modspec(4), modspec(5),
            _resident((1, D_MODEL)),
            _resident((D_MODEL, 2 * D_FF)), _weight((D_FF, D_MODEL)),
            _resident((1, D_MODEL)),
        ],
        out_specs=pl.BlockSpec((tm, D_MODEL), lambda i: (i, 0)),
        out_shape=jax.ShapeDtypeStruct((t, D_MODEL), F32),
        scratch_shapes=[pltpu.VMEM((tm, D_FF), BF16)],
        compiler_params=pltpu.CompilerParams(
            dimension_semantics=("arbitrary",), vmem_limit_bytes=VMEM_LIMIT),
        name="ffn",
    )(x2d, mod_arr, mod_arr, mod_arr, lw["n2w"], lw["wffn_in"], lw["wffn_out"], fnw)


def _mix_in_sample_kernel(x_ref, sh_ref, sc_ref, n1w_ref, wu_ref, wzh_ref, wx_ref, wg_ref, wdt_ref,
                          dtb_ref, wpool_ref, pscale_ref, cw_ref, cb_ref, spool_ref, sconv_ref,
                          ag_ref, gb_ref, sz_ref, xc_ref, dt_ref, pnew_ref, cnew_ref):
    hb = _norm_mod(x_ref[...], sh_ref[...], sc_ref[...], n1w_ref[...])
    u = _dot(hb, wu_ref[:, 0:D_MODEL])
    pnew_ref[0:POOL_BUF - 1] = spool_ref[1:POOL_BUF]
    pnew_ref[POOL_BUF - 1] = u
    sz_ref[...] = _silu_of_half(_dot(hb, wzh_ref[:, 0:D_INNER]))
    xbc = _dot(hb, wx_ref[:, 0:CONV_DIM])
    cnew_ref[0:CONV_W - 2] = sconv_ref[1:CONV_W - 1]
    cnew_ref[CONV_W - 2] = xbc
    gates = jax.nn.sigmoid(_dot(hb, wg_ref[:, 0:2 * D_MODEL]))
    gb_ref[...] = gates[:, D_MODEL:]
    dt_ref[...] = _softplus(_dot(hb, wdt_ref[...]) + dtb_ref[...])

    for g, w in enumerate(POOL_WINDOWS):
        sl = slice(g * POOL_GW, (g + 1) * POOL_GW)
        cur = u[:, sl]
        acc = cur
        for k in range(1, w):
            acc = acc + spool_ref[POOL_BUF - k, :, sl]
        cnt = float(min(PAST_LEN + 1, w))
        a = _pool_group_out(acc / cnt - cur, g, wpool_ref, pscale_ref)
        ag_ref[:, sl] = gates[:, sl] * a

    acc = cb_ref[...] + xbc * cw_ref[CONV_W - 1:CONV_W, :]
    for k in range(CONV_W - 1):
        acc = acc + sconv_ref[k] * cw_ref[k:k + 1, :]
    xc_ref[...] = _silu(acc)


def _mix_in_sample(x2d, mod_s, lw, spool_t, sconv_t):
    n = x2d.shape[0]
    full = lambda r, c: pl.BlockSpec((r, c), lambda i: (0, 0), pipeline_mode=pl.Buffered(1))
    full3 = lambda a, r, c: pl.BlockSpec((a, r, c), lambda i: (0, 0, 0), pipeline_mode=pl.Buffered(1))
    modspec = lambda k: pl.BlockSpec((n, D_MODEL), lambda i, k=k: (0, k),
                                     pipeline_mode=pl.Buffered(1))
    return pl.pallas_call(
        _mix_in_sample_kernel,
        grid=(1,),
        in_specs=[
            full(n, D_MODEL), modspec(0), modspec(1),
            _resident((1, D_MODEL)),
            _weight((D_MODEL, D_MODEL)), _weight((D_MODEL, D_INNER)),
            _weight((D_MODEL, CONV_DIM)), _weight((D_MODEL, 2 * D_MODEL)),
            _resident((D_MODEL, LANES)), _resident((1, LANES)),
            _resident((len(POOL_WINDOWS), POOL_GW, POOL_GW)), _resident((1, D_MODEL)),
            _resident((CONV_W, CONV_DIM)), _resident((1, CONV_DIM)),
            full3(POOL_BUF, n, D_MODEL), full3(CONV_W - 1, n, CONV_DIM),
        ],
        out_specs=[
            full(n, D_MODEL), full(n, D_MODEL), full(n, D_INNER), full(n, CONV_DIM), full(n, LANES),
            full3(POOL_BUF, n, D_MODEL), full3(CONV_W - 1, n, CONV_DIM),
        ],
        out_shape=[
            jax.ShapeDtypeStruct((n, D_MODEL), F32),
            jax.ShapeDtypeStruct((n, D_MODEL), F32),
            jax.ShapeDtypeStruct((n, D_INNER), F32),
            jax.ShapeDtypeStruct((n, CONV_DIM), F32),
            jax.ShapeDtypeStruct((n, LANES), F32),
            jax.ShapeDtypeStruct((POOL_BUF, n, D_MODEL), F32),
            jax.ShapeDtypeStruct((CONV_W - 1, n, CONV_DIM), F32),
        ],
        compiler_params=pltpu.CompilerParams(
            dimension_semantics=("arbitrary",), vmem_limit_bytes=VMEM_LIMIT),
        name="mix_in_sample",
    )(x2d, mod_s, mod_s, lw["n1w"], lw["wu"], lw["wzh"], lw["wx"], lw["wg"], lw["wdt"], lw["dtb"],
      lw["wpool"], lw["pscale"], lw["cw"], lw["cb"], spool_t, sconv_t)


def _ssm_step_kernel(h_ref, xc_ref, dt_ref, alog_ref, dexp_ref, hout_ref, y_ref, *, bb):
    lane = lax.broadcasted_iota(jnp.int32, (bb, LANES), 1)
    head_ok = lane < HEADS
    dt = jnp.where(head_ok, dt_ref[...], 0.0)
    a_neg = jnp.where(head_ok[0:1, :], -jnp.exp(alog_ref[...]), 0.0)
    dec = jnp.exp(dt * a_neg)
    xs = xc_ref[:, 0:D_INNER]
    dt_exp = jnp.concatenate([_group_expand(dt, g) for g in range(GROUPS)], axis=1)
    pad = jnp.zeros((bb, D_INNER), F32)
    xdt = jnp.concatenate([xs * dt_exp, pad], axis=0).astype(BF16)
    bm = xc_ref[:, D_INNER:D_INNER + GROUPS * D_STATE]
    cm = xc_ref[:, D_INNER + GROUPS * D_STATE:]
    rowid = lax.broadcasted_iota(jnp.int32, (2 * bb, D_STATE), 0)
    rowid_y = lax.broadcasted_iota(jnp.int32, (2 * bb, GROUP_W), 0)
    padn = jnp.zeros((bb, D_STATE), F32)

    y_acc = [jnp.zeros((2 * bb, GROUP_W), F32) for _ in range(GROUPS)]
    for j in range(bb):
        for g in range(GROUPS):
            rs = slice(g * GROUP_W, (g + 1) * GROUP_W)
            ns = slice(g * D_STATE, (g + 1) * D_STATE)
            h_g = h_ref[j, rs, :]
            dcol = jnp.concatenate(
                [jnp.broadcast_to(dec[j:j + 1, HPG * g + r:HPG * g + r + 1], (HEADDIM, D_STATE))
                 for r in range(HPG)], axis=0)
            b16 = jnp.concatenate([bm[:, ns], padn], axis=0)
            b_j = jnp.where(rowid == j, b16, 0.0).astype(BF16)
            new = h_g * dcol + _dot_tn(xdt[:, rs], b_j)
            hout_ref[j, rs, :] = new
            c16 = jnp.concatenate([cm[:, ns], padn], axis=0).astype(BF16)
            y_all = _dot_nt(c16, new.astype(BF16))
            y_acc[g] = y_acc[g] + jnp.where(rowid_y == j, y_all, 0.0)
    y = jnp.concatenate(y_acc, axis=1)[0:bb, :]
    y_ref[...] = y + dexp_ref[...] * xs


def _ssm_step(h3, xc, dt, lw, bb=8):
    n = h3.shape[0]
    return pl.pallas_call(
        functools.partial(_ssm_step_kernel, bb=bb),
        grid=(n // bb,),
        in_specs=[
            pl.BlockSpec((bb, HEADS * HEADDIM, D_STATE), lambda i: (i, 0, 0)),
            pl.BlockSpec((bb, CONV_DIM), lambda i: (i, 0)),
            pl.BlockSpec((bb, LANES), lambda i: (i, 0)),
            _resident((1, LANES)), _resident((1, D_INNER)),
        ],
        out_specs=[
            pl.BlockSpec((bb, HEADS * HEADDIM, D_STATE), lambda i: (i, 0, 0)),
            pl.BlockSpec((bb, D_INNER), lambda i: (i, 0)),
        ],
        out_shape=[
            jax.ShapeDtypeStruct(h3.shape, F32),
            jax.ShapeDtypeStruct((n, D_INNER), F32),
        ],
        compiler_params=pltpu.CompilerParams(
            dimension_semantics=("arbitrary",), vmem_limit_bytes=VMEM_LIMIT),
        name="ssm_step",
    )(h3, xc, dt, lw["alog"], lw["dexp"])


def _mix_out_sample_kernel(y_ref, sz_ref, ag_ref, gb_ref, x_ref, g1_ref, nw_ref, wproj_ref, wout_ref,
                           o_ref):
    o_ref[...] = _mixer_tail(y_ref[...], sz_ref[...], ag_ref[...], gb_ref[...], x_ref[...],
                             g1_ref[...], nw_ref[...], wproj_ref, wout_ref)


def _mix_out_sample(y, sz, ag, gb, x2d, mod_s, lw):
    n = x2d.shape[0]
    full = lambda c: pl.BlockSpec((n, c), lambda i: (0, 0))
    return pl.pallas_call(
        _mix_out_sample_kernel,
        grid=(1,),
        in_specs=[
            full(D_INNER), full(D_INNER), full(D_MODEL), full(D_MODEL), full(D_MODEL),
            pl.BlockSpec((n, D_MODEL), lambda i: (0, 2)),
            _resident((1, D_INNER)), _weight((D_INNER, D_MODEL)), _weight((D_MODEL, D_MODEL)),
        ],
        out_specs=full(D_MODEL),
        out_shape=jax.ShapeDtypeStruct((n, D_MODEL), F32),
        compiler_params=pltpu.CompilerParams(
            dimension_semantics=("arbitrary",), vmem_limit_bytes=VMEM_LIMIT),
        name="mix_out_sample",
    )(y, sz, ag, gb, x2d, mod_s, lw["ssd_nw"], lw["wproj"], lw["wout"])


def _layer_weights(l, w_in, w_pool, pool_scale, conv_w, conv_b, dt_bias, A_log, D_skip, ssd_norm_w,
                   w_ssd_proj, w_out, norm1_w, norm2_w, w_ffn_in, w_ffn_out):
    rep = LANES // HEADS
    lane_id = jnp.arange(LANES)[:, None]
    col_head = jnp.arange(D_INNER)[None, :] // HEADDIM
    hexp = ((lane_id % HEADS == col_head) & (lane_id < 3 * HEADS)).astype(BF16)
    return dict(
        n1w=norm1_w[l].reshape(1, D_MODEL),
        **_prep_w_in(w_in, l),
        dtb=jnp.tile(dt_bias[l], rep).reshape(1, LANES),
        wpool=w_pool[l].astype(BF16),
        pscale=pool_scale[l].reshape(1, D_MODEL),
        cw=conv_w[l],
        cb=conv_b[l].reshape(1, CONV_DIM),
        alog=jnp.tile(A_log[l], rep).reshape(1, LANES),
        hexp=hexp,
        dexp=jnp.repeat(D_skip[l], HEADDIM).reshape(1, D_INNER),
        ssd_nw=ssd_norm_w[l].reshape(1, D_INNER),
        wproj=_cast_weight(w_ssd_proj, l),
        wout=_cast_weight(w_out, l),
        n2w=norm2_w[l].reshape(1, D_MODEL),
        wffn_in=_cast_weight(w_ffn_in, l, pad=0),
        wffn_out=_cast_weight(w_ffn_out, l),
    )


def kernel(x_prompt, x_sample, c_prompt, c_sample, state_pool, state_conv, state_ssm, w_ada, b_ada, norm1_w, w_in, w_pool, pool_scale, conv_w, conv_b, dt_bias, A_log, D_skip, ssd_norm_w, w_ssd_proj, w_out, norm2_w, w_ffn_in, w_ffn_out, final_norm_w):
    depth = w_in.shape[0]
    bsz, L, _ = x_prompt.shape
    nsmp = x_sample.shape[0]
    xp = x_prompt
    xs = x_sample.reshape(nsmp, D_MODEL)
    c_all = jnp.concatenate([c_prompt, c_sample], axis=0)
    fnw = final_norm_w.reshape(1, D_MODEL)
    pool_p, conv_p, ssm_p, pool_s, conv_s, ssm_s = [], [], [], [], [], []
    for l in range(depth):
        final = l == depth - 1
        lw = _layer_weights(l, w_in, w_pool, pool_scale, conv_w, conv_b, dt_bias, A_log, D_skip,
                            ssd_norm_w, w_ssd_proj, w_out, norm1_w, norm2_w, w_ffn_in, w_ffn_out)
        mod = _modulation(c_all, w_ada[l], b_ada[l])
        mod_p = mod[:bsz].reshape(bsz, 1, N_MOD * D_MODEL)
        mod_s = mod[bsz:]

        ag, gb, sz, xc, dt, pst, cst = _mix_in_prompt(xp, mod_p, lw)
        x1, hst = _ssd_prompt(xc, dt, sz, ag, gb, xp, mod_p, lw)
        tm = 256
        per = L // tm
        p_mods = (mod_p, lambda k: pl.BlockSpec((None, 1, D_MODEL), lambda i, k=k: (i // per, 0, k)))
        xp = _ffn(x1.reshape(bsz * L, D_MODEL), p_mods, lw, fnw, final, tm).reshape(bsz, L, D_MODEL)
        pool_p.append(pst)
        conv_p.append(cst)
        ssm_p.append(hst.reshape(bsz, HEADS, HEADDIM, D_STATE))

        ag, gb, sz, xc, dt, pnew_t, cnew_t = _mix_in_sample(
            xs, mod_s, lw, jnp.swapaxes(state_pool[l], 0, 1), jnp.swapaxes(state_conv[l], 0, 1))
        hnew, y = _ssm_step(state_ssm[l].reshape(nsmp, HEADS * HEADDIM, D_STATE), xc, dt, lw)
        x1s = _mix_out_sample(y, sz, ag, gb, xs, mod_s, lw)
        s_mods = (mod_s, lambda k: pl.BlockSpec((nsmp, D_MODEL), lambda i, k=k: (0, k)))
        xs = _ffn(x1s, s_mods, lw, fnw, final, nsmp)
        pool_s.append(jnp.swapaxes(pnew_t, 0, 1))
        conv_s.append(jnp.swapaxes(cnew_t, 0, 1))
        ssm_s.append(hnew.reshape(nsmp, HEADS, HEADDIM, D_STATE))

    return (xp, xs.reshape(nsmp, 1, D_MODEL), jnp.stack(pool_p), jnp.stack(conv_p), jnp.stack(ssm_p),
            jnp.stack(pool_s), jnp.stack(conv_s), jnp.stack(ssm_s))
```

```python
import functools

import jax
import jax.numpy as jnp
from jax import lax
from jax.experimental import pallas as pl
from jax.experimental.pallas import tpu as pltpu

D_MODEL = 1024
POOL_WINDOWS = (2, 4, 8, 16)
POOL_GW = D_MODEL // len(POOL_WINDOWS)
POOL_BUF = max(POOL_WINDOWS) - 1
D_INNER = 2 * D_MODEL
HEADDIM = 64
HEADS = D_INNER // HEADDIM
GROUPS = 8
HPG = HEADS // GROUPS
D_STATE = 128
CONV_W = 4
CONV_DIM = D_INNER + 2 * GROUPS * D_STATE
CHUNK = 128
D_FF = ((8 * D_MODEL // 3 + 255) // 256) * 256
N_MOD = 6
COL_Z = D_MODEL
COL_XBC = COL_Z + D_INNER
COL_DT = COL_XBC + CONV_DIM
COL_GATE = COL_DT + HEADS
IN_COLS = COL_GATE + 2 * D_MODEL
PAST_LEN = 16384
EPS = 1e-6

LANES = 128
SUBLANES = 8
MXU_N = 256
WPAD = LANES
GROUP_W = HPG * HEADDIM
POOL_HALO = 16
CONV_HALO = 8
CONV_ROWS = 64
VMEM_LIMIT = 60 * 1024 * 1024

BF16 = jnp.bfloat16
F32 = jnp.float32


def _dot(a, b):
    return jnp.dot(a, b, preferred_element_type=F32)


def _dot_nt(a, b):
    return lax.dot_general(a, b, (((1,), (1,)), ((), ())), preferred_element_type=F32)


def _dot_tn(a, b):
    return lax.dot_general(a, b, (((0,), (0,)), ((), ())), preferred_element_type=F32)


def _silu_of_half(h):
    return h * jnp.tanh(h) + h


def _silu(x):
    return _silu_of_half(0.5 * x)


def _rms(x, w):
    return x * lax.rsqrt(jnp.mean(x * x, axis=-1, keepdims=True) + EPS) * w


def _softplus(x):
    return jnp.maximum(x, 0.0) + jnp.log1p(jnp.exp(-jnp.abs(x)))


def _resident(shape):
    nd = len(shape)
    return pl.BlockSpec(shape, lambda *_: (0,) * nd, pipeline_mode=pl.Buffered(1))


def _weight(shape):
    return _resident((shape[0], shape[1] + WPAD))


def _cast_weight_kernel(w_ref, o_ref):
    n = w_ref.shape[1]
    o_ref[:, 0:n] = w_ref[...].astype(BF16)
    if o_ref.shape[1] > n:
        o_ref[:, n:] = jnp.zeros((o_ref.shape[0], o_ref.shape[1] - n), BF16)


def _cast_weight(w, l, pad=WPAD, steps=4):
    _, k, n = w.shape
    rows = k // steps
    return pl.pallas_call(
        _cast_weight_kernel,
        grid=(k // rows,),
        in_specs=[pl.BlockSpec((None, rows, n), lambda i: (l, i, 0))],
        out_specs=pl.BlockSpec((rows, n + pad), lambda i: (i, 0)),
        out_shape=jax.ShapeDtypeStruct((k, n + pad), BF16),
        compiler_params=pltpu.CompilerParams(dimension_semantics=("arbitrary",)),
        name="cast_weight",
    )(w)


def _mod_kernel(c_ref, w_ref, b_ref, o_ref):
    a = _silu(c_ref[...]).astype(BF16)
    o_ref[...] = _dot(a, w_ref[...].astype(BF16)) + b_ref[...]


def _modulation(c, w_ada, b_ada):
    m = c.shape[0]
    n = w_ada.shape[1]
    tn = 512
    return pl.pallas_call(
        _mod_kernel,
        grid=(n // tn,),
        in_specs=[
            pl.BlockSpec((m, D_MODEL), lambda j: (0, 0)),
            pl.BlockSpec((D_MODEL, tn), lambda j: (0, j)),
            pl.BlockSpec((1, tn), lambda j: (0, j)),
        ],
        out_specs=pl.BlockSpec((m, tn), lambda j: (0, j)),
        out_shape=jax.ShapeDtypeStruct((m, n), F32),
        name="adaln_mod",
    )(c, w_ada, b_ada.reshape(1, n))


WCOL_G = COL_DT
W_COLS = WCOL_G + 2 * D_MODEL
PREP_ROWS = LANES


def _prep_w_in_kernel(wt_ref, wt_dt_ref, w_ref, wdt_ref):
    i = pl.program_id(0)
    last = pl.num_programs(0) - 1
    halve = (i >= COL_Z // PREP_ROWS) & (i < COL_XBC // PREP_ROWS)
    scale = jnp.where(halve, 0.5, 1.0)

    @pl.when(i < last)
    def _():
        w_ref[...] = (scale * wt_ref[0]).astype(BF16).T

    @pl.when(i == last)
    def _():
        w_ref[...] = jnp.zeros_like(w_ref)

    @pl.when(i == 0)
    def _():
        d = wt_dt_ref[0].T[:, 0:HEADS]
        wdt_ref[...] = jnp.concatenate([d] * (LANES // HEADS), axis=1).astype(BF16)


def _prep_w_in(w_in, l):
    wt = jnp.swapaxes(w_in, 1, 2)
    n_main = COL_DT // PREP_ROWS
    n_blocks = W_COLS // PREP_ROWS
    elem_block = (pl.Element(1), pl.Element(PREP_ROWS), pl.Element(D_MODEL))

    def src_row(i):
        j = jnp.minimum(i, n_blocks - 1)
        per = PREP_ROWS // SUBLANES
        tile_row = jnp.where(j < n_main, j * per, COL_GATE // SUBLANES + (j - n_main) * per)
        return tile_row * SUBLANES

    w_all, wdt = pl.pallas_call(
        _prep_w_in_kernel,
        grid=(n_blocks + WPAD // PREP_ROWS,),
        in_specs=[
            pl.BlockSpec(elem_block, lambda i: (l, src_row(i), 0)),
            pl.BlockSpec(elem_block, lambda i: (l, COL_DT, 0)),
        ],
        out_specs=[
            pl.BlockSpec((D_MODEL, PREP_ROWS), lambda i: (0, i)),
            pl.BlockSpec((D_MODEL, LANES), lambda i: (0, 0)),
        ],
        out_shape=[
            jax.ShapeDtypeStruct((D_MODEL, W_COLS + WPAD), BF16),
            jax.ShapeDtypeStruct((D_MODEL, LANES), BF16),
        ],
        compiler_params=pltpu.CompilerParams(dimension_semantics=("arbitrary",)),
        name="prep_w_in",
    )(wt, wt)
    return dict(w_all=w_all, wdt=wdt)


def _norm_mod(x, sh, sc, w):
    return (_rms(x, w) * (1.0 + sc) + sh).astype(BF16)


def _pool_group_out(pooled, g, wpool_ref, pscale_ref):
    sl = slice(g * POOL_GW, (g + 1) * POOL_GW)
    return _dot(pooled.astype(BF16), wpool_ref[g]) * pscale_ref[:, sl]


def _mix_in_prompt_kernel(x_ref, sh_ref, sc_ref, n1w_ref, w_ref, wdt_ref,
                          dtb_ref, wpool_ref, pscale_ref, cw_ref, cb_ref,
                          ag_ref, gb_ref, sz_ref, xc_ref, dt_ref, pst_ref, cst_ref,
                          ubuf, xbuf, *, tm):
    i = pl.program_id(1)

    @pl.when(i == 0)
    def _():
        ubuf[0:POOL_HALO, :] = jnp.zeros((POOL_HALO, D_MODEL), F32)
        xbuf[0:CONV_HALO, :] = jnp.zeros((CONV_HALO, CONV_DIM), F32)

    @pl.when(i > 0)
    def _():
        ubuf[0:POOL_HALO, :] = ubuf[tm:tm + POOL_HALO, :]
        xbuf[0:CONV_HALO, :] = xbuf[tm:tm + CONV_HALO, :]

    hb = _norm_mod(x_ref[...], sh_ref[...], sc_ref[...], n1w_ref[...])
    dt_ref[...] = _softplus(_dot(hb, wdt_ref[...]) + dtb_ref[...])

    pos = i * tm + lax.broadcasted_iota(jnp.int32, (tm, POOL_GW), 0)

    def pool_group(g):
        w = POOL_WINDOWS[g]
        sl = slice(g * POOL_GW, (g + 1) * POOL_GW)
        ubuf[POOL_HALO:POOL_HALO + tm, sl] = _dot(hb, w_ref[:, sl])
        full = ubuf[:, sl]
        acc = full
        span = 1
        while span < w:
            acc = acc + pltpu.roll(acc, span, axis=0)
            span *= 2
        cur = full[POOL_HALO:, :]
        cnt = jnp.minimum(pos + 1, w).astype(F32)
        a = _pool_group_out(acc[POOL_HALO:, :] / cnt - cur, g, wpool_ref, pscale_ref)
        gate_a = _dot(hb, w_ref[:, WCOL_G + g * POOL_GW:WCOL_G + (g + 1) * POOL_GW])
        ag_ref[:, sl] = jax.nn.sigmoid(gate_a) * a

    cchunk = 1024

    def gate_b_chunk(c0):
        gb_ref[:, c0:c0 + cchunk] = jax.nn.sigmoid(
            _dot(hb, w_ref[:, WCOL_G + D_MODEL + c0:WCOL_G + D_MODEL + c0 + cchunk]))

    def z_chunk(c0):
        sz_ref[:, c0:c0 + cchunk] = _silu_of_half(_dot(hb, w_ref[:, COL_Z + c0:COL_Z + c0 + cchunk]))

    def conv_chunk(c0):
        for n0 in range(c0, c0 + cchunk, MXU_N):
            xbuf[CONV_HALO:CONV_HALO + tm, n0:n0 + MXU_N] = _dot(
                hb, w_ref[:, COL_XBC + n0:COL_XBC + n0 + MXU_N])
        for l0 in range(c0, c0 + cchunk, LANES):
            sl = slice(l0, l0 + LANES)
            taps = 0.5 * cw_ref[:, sl]
            bias = 0.5 * cb_ref[:, sl]
            for r0 in range(0, tm, CONV_ROWS):
                src = xbuf[r0:r0 + CONV_HALO + CONV_ROWS, sl]
                acc = bias + src[CONV_HALO:, :] * taps[CONV_W - 1:CONV_W, :]
                for k in range(CONV_W - 1):
                    back = CONV_W - 1 - k
                    acc = acc + pltpu.roll(src, back, axis=0)[CONV_HALO:, :] * taps[k:k + 1, :]
                xc_ref[r0:r0 + CONV_ROWS, sl] = _silu_of_half(acc)

    light = ([functools.partial(pool_group, g) for g in range(len(POOL_WINDOWS))]
             + [functools.partial(gate_b_chunk, c0) for c0 in range(0, D_MODEL, cchunk)]
             + [functools.partial(z_chunk, c0) for c0 in range(0, D_INNER, cchunk)])
    heavy = [functools.partial(conv_chunk, c0) for c0 in range(0, CONV_DIM, cchunk)]
    for k in range(max(len(light), len(heavy))):
        if k < len(light):
            light[k]()
        if k < len(heavy):
            heavy[k]()

    @pl.when(i == pl.num_programs(1) - 1)
    def _():
        pst_ref[...] = ubuf[POOL_HALO + tm - POOL_BUF:POOL_HALO + tm, :]
        cst_ref[...] = xbuf[CONV_HALO + tm - (CONV_W - 1):CONV_HALO + tm, :]


def _mix_in_prompt(x, mod3, lw, tm=256):
    bsz, L, _ = x.shape
    nt = L // tm
    tok = lambda w: pl.BlockSpec((None, tm, w), lambda b, i: (b, i, 0))
    modspec = lambda k: pl.BlockSpec((None, 1, D_MODEL), lambda b, i, k=k: (b, 0, k))
    outs = pl.pallas_call(
        functools.partial(_mix_in_prompt_kernel, tm=tm),
        grid=(bsz, nt),
        in_specs=[
            tok(D_MODEL), modspec(0), modspec(1),
            _resident((1, D_MODEL)),
            _weight((D_MODEL, W_COLS)),
            _resident((D_MODEL, LANES)), _resident((1, LANES)),
            _resident((len(POOL_WINDOWS), POOL_GW, POOL_GW)), _resident((1, D_MODEL)),
            _resident((CONV_W, CONV_DIM)), _resident((1, CONV_DIM)),
        ],
        out_specs=[
            tok(D_MODEL), tok(D_MODEL), tok(D_INNER), tok(CONV_DIM), tok(LANES),
            pl.BlockSpec((None, POOL_BUF, D_MODEL), lambda b, i: (b, 0, 0)),
            pl.BlockSpec((None, CONV_W - 1, CONV_DIM), lambda b, i: (b, 0, 0)),
        ],
        out_shape=[
            jax.ShapeDtypeStruct((bsz, L, D_MODEL), F32),
            jax.ShapeDtypeStruct((bsz, L, D_MODEL), F32),
            jax.ShapeDtypeStruct((bsz, L, D_INNER), F32),
            jax.ShapeDtypeStruct((bsz, L, CONV_DIM), F32),
            jax.ShapeDtypeStruct((bsz, L, LANES), F32),
            jax.ShapeDtypeStruct((bsz, POOL_BUF, D_MODEL), F32),
            jax.ShapeDtypeStruct((bsz, CONV_W - 1, CONV_DIM), F32),
        ],
        scratch_shapes=[
            pltpu.VMEM((POOL_HALO + tm, D_MODEL), F32),
            pltpu.VMEM((CONV_HALO + tm, CONV_DIM), F32),
        ],
        compiler_params=pltpu.CompilerParams(
            dimension_semantics=("arbitrary", "arbitrary"), vmem_limit_bytes=VMEM_LIMIT),
        name="mix_in_prompt",
    )(x, mod3, mod3, lw["n1w"], lw["w_all"], lw["wdt"], lw["dtb"],
      lw["wpool"], lw["pscale"], lw["cw"], lw["cb"])
    return outs


def _mixer_tail(y, sz, ag, gb, x, g1, nw, wproj_ref, wout_ref):
    yn = _rms(y * sz, nw).astype(BF16)
    b_out = _dot(yn, wproj_ref[:, 0:D_MODEL])
    mix = (ag + gb * b_out).astype(BF16)
    return x + g1 * _dot(mix, wout_ref[:, 0:D_MODEL])


def _head_pair_expand(v, h0):
    rows = v.shape[0]
    lane = lax.broadcasted_iota(jnp.int32, (rows, LANES), 1)
    lo = jnp.broadcast_to(v[:, h0:h0 + 1], (rows, LANES))
    hi = jnp.broadcast_to(v[:, h0 + 1:h0 + 2], (rows, LANES))
    return jnp.where(lane < HEADDIM, lo, hi)


def _group_expand(v, g):
    return jnp.concatenate(
        [_head_pair_expand(v, HPG * g + 2 * t) for t in range(GROUP_W // LANES)], axis=1)


def _ssd_prompt_kernel(xc_ref, dt_ref, sz_ref, ag_ref, gb_ref, x_ref, g1_ref, alog_ref, dexp_ref,
                       hexp_ref, nw_ref, wproj_ref, wout_ref,
                       x1_ref, hout_ref, ht_scr, y_scr, *, rows):
    step = pl.program_id(1)

    @pl.when(step == 0)
    def _():
        ht_scr[...] = jnp.zeros_like(ht_scr)

    q = CHUNK
    row = lax.broadcasted_iota(jnp.int32, (q, q), 0)
    lane = lax.broadcasted_iota(jnp.int32, (q, q), 1)
    causal = row >= lane
    tri = causal.astype(F32)
    a_neg = -jnp.exp(alog_ref[...])
    lane_g = lax.broadcasted_iota(jnp.int32, (q, GROUP_W), 1)

    for r0 in range(0, rows, q):
        rs = slice(r0, r0 + q)
        dt = dt_ref[rs, :]
        acs = lax.dot_general(tri, dt * a_neg, (((1,), (0,)), ((), ())),
                              precision=lax.Precision.HIGHEST, preferred_element_type=F32)
        acs_t = acs.T
        dt_t = dt.T
        w_t = dt_t * jnp.exp(acs_t[:, q - 1:q] - acs_t)

        e_acs = jnp.exp(acs)
        e_hi = e_acs.astype(BF16).astype(F32)
        e_r1 = e_acs - e_hi
        e_mid = e_r1.astype(BF16).astype(F32)
        e_lo = e_r1 - e_mid
        pieces = jnp.where(lane < HEADS, e_hi,
                           jnp.where(lane < 2 * HEADS, e_mid,
                                     jnp.where(lane < 3 * HEADS, e_lo, 0.0))).astype(BF16)
        e_exp = _dot(pieces, hexp_ref[...])

        c_bf, bt_f, cbs = [], [], []
        for g in range(GROUPS):
            b0 = D_INNER + g * D_STATE
            c0 = D_INNER + GROUPS * D_STATE + g * D_STATE
            bt = xc_ref[rs, b0:b0 + D_STATE].T
            cg = xc_ref[rs, c0:c0 + D_STATE].astype(BF16)
            c_bf.append(cg)
            bt_f.append(bt)
            cbs.append(_dot(cg, bt.astype(BF16)))

        for g in range(GROUPS):
            gs = slice(g * GROUP_W, (g + 1) * GROUP_W)
            x_g = xc_ref[rs, gs]
            scores, btw, blocks = [], [], []
            for r in range(HPG):
                h = HPG * g + r
                seg = acs[:, h:h + 1] - acs_t[h:h + 1, :]
                decay = jnp.exp(jnp.where(causal, seg, -jnp.inf))
                scores.append((cbs[g] * decay * dt_t[h:h + 1, :]).astype(BF16))
                btw.append((bt_f[g] * w_t[h:h + 1, :]).astype(BF16))
                in_head = (lane_g >= r * HEADDIM) & (lane_g < (r + 1) * HEADDIM)
                blocks.append(jnp.where(in_head, x_g, 0.0).astype(BF16))
            lhs = jnp.concatenate(
                [jnp.concatenate(scores, axis=1), jnp.concatenate(btw, axis=1)], axis=0)
            both = _dot(lhs, jnp.concatenate(blocks, axis=0))
            ht_prev = ht_scr[g * D_STATE:(g + 1) * D_STATE, :]
            y_off = _dot(c_bf[g], ht_prev.astype(BF16)) * e_exp[:, gs]
            y_scr[rs, gs] = both[0:q, :] + y_off + dexp_ref[:, gs] * x_g
            ht_scr[g * D_STATE:(g + 1) * D_STATE, :] = (
                ht_prev * e_exp[q - 1:q, gs] + both[q:2 * q, :])

    x1_ref[...] = _mixer_tail(y_scr[...], sz_ref[...], ag_ref[...], gb_ref[...], x_ref[...],
                              g1_ref[...], nw_ref[...], wproj_ref, wout_ref)

    @pl.when(step == pl.num_programs(1) - 1)
    def _():
        for g in range(GROUPS):
            hout_ref[g * GROUP_W:(g + 1) * GROUP_W, :] = ht_scr[g * D_STATE:(g + 1) * D_STATE, :].T


def _ssd_prompt(xc, dt, sz, ag, gb, x, mod3, lw, rows=4 * CHUNK):
    bsz, L, _ = x.shape
    tok = lambda w: pl.BlockSpec((None, rows, w), lambda b, c: (b, c, 0))
    return pl.pallas_call(
        functools.partial(_ssd_prompt_kernel, rows=rows),
        grid=(bsz, L // rows),
        in_specs=[
            tok(CONV_DIM), tok(LANES), tok(D_INNER), tok(D_MODEL), tok(D_MODEL), tok(D_MODEL),
            pl.BlockSpec((None, 1, D_MODEL), lambda b, c: (b, 0, 2)),
            _resident((1, LANES)), _resident((1, D_INNER)), _resident((LANES, D_INNER)),
            _resident((1, D_INNER)),
            _weight((D_INNER, D_MODEL)), _weight((D_MODEL, D_MODEL)),
        ],
        out_specs=[
            tok(D_MODEL),
            pl.BlockSpec((None, HEADS * HEADDIM, D_STATE), lambda b, c: (b, 0, 0)),
        ],
        out_shape=[
            jax.ShapeDtypeStruct((bsz, L, D_MODEL), F32),
            jax.ShapeDtypeStruct((bsz, HEADS * HEADDIM, D_STATE), F32),
        ],
        scratch_shapes=[
            pltpu.VMEM((GROUPS * D_STATE, GROUP_W), F32),
            pltpu.VMEM((rows, D_INNER), F32),
        ],
        compiler_params=pltpu.CompilerParams(
            dimension_semantics=("arbitrary", "arbitrary"), vmem_limit_bytes=VMEM_LIMIT),
        name="ssd_prompt",
    )(xc, dt, sz, ag, gb, x, mod3, lw["alog"], lw["dexp"], lw["hexp"], lw["ssd_nw"], lw["wproj"],
      lw["wout"])


def _ffn_kernel(x_ref, sh_ref, sc_ref, g2_ref, n2w_ref, win_ref, wout_ref, fnw_ref, o_ref, act_scr,
                *, final):
    x = x_ref[...]
    hb = _norm_mod(x, sh_ref[...], sc_ref[...], n2w_ref[...])
    cw = 256
    for c0 in range(0, D_FF, cw):
        gt = _dot(hb, win_ref[:, c0:c0 + cw])
        up = _dot(hb, win_ref[:, D_FF + c0:D_FF + c0 + cw])
        act_scr[:, c0:c0 + cw] = (_silu(gt) * up).astype(BF16)
    x2 = x + g2_ref[...] * _dot(act_scr[...], wout_ref[:, 0:D_MODEL])
    o_ref[...] = _rms(x2, fnw_ref[...]) if final else x2


def _ffn(x2d, mods, lw, fnw, final, tm):
    t = x2d.shape[0]
    mod_arr, modspec = mods
    return pl.pallas_call(
        functools.partial(_ffn_kernel, final=final),
        grid=(t // tm,),
        in_specs=[
            pl.BlockSpec((tm, D_MODEL), lambda i: (i, 0)),
            modspec(3), modspec(4), modspec(5),
            _resident((1, D_MODEL)),
            _resident((D_MODEL, 2 * D_FF)), _weight((D_FF, D_MODEL)),
            _resident((1, D_MODEL)),
        ],
        out_specs=pl.BlockSpec((tm, D_MODEL), lambda i: (i, 0)),
        out_shape=jax.ShapeDtypeStruct((t, D_MODEL), F32),
        scratch_shapes=[pltpu.VMEM((tm, D_FF), BF16)],
        compiler_params=pltpu.CompilerParams(
            dimension_semantics=("arbitrary",), vmem_limit_bytes=VMEM_LIMIT),
        name="ffn",
    )(x2d, mod_arr, mod_arr, mod_arr, lw["n2w"], lw["wffn_in"], lw["wffn_out"], fnw)


def _mix_in_sample_kernel(x_ref, sh_ref, sc_ref, n1w_ref, w_ref, wdt_ref,
                          dtb_ref, wpool_ref, pscale_ref, cw_ref, cb_ref, spool_ref, sconv_ref,
                          ag_ref, gb_ref, sz_ref, xc_ref, dt_ref, pnew_ref, cnew_ref):
    hb = _norm_mod(x_ref[...], sh_ref[...], sc_ref[...], n1w_ref[...])
    u = _dot(hb, w_ref[:, 0:COL_Z])
    pnew_ref[0:POOL_BUF - 1] = spool_ref[1:POOL_BUF]
    pnew_ref[POOL_BUF - 1] = u
    sz_ref[...] = _silu_of_half(_dot(hb, w_ref[:, COL_Z:COL_XBC]))
    xbc = _dot(hb, w_ref[:, COL_XBC:COL_DT])
    cnew_ref[0:CONV_W - 2] = sconv_ref[1:CONV_W - 1]
    cnew_ref[CONV_W - 2] = xbc
    gates = jax.nn.sigmoid(_dot(hb, w_ref[:, WCOL_G:W_COLS]))
    gb_ref[...] = gates[:, D_MODEL:]
    dt_ref[...] = _softplus(_dot(hb, wdt_ref[...]) + dtb_ref[...])

    for g, w in enumerate(POOL_WINDOWS):
        sl = slice(g * POOL_GW, (g + 1) * POOL_GW)
        cur = u[:, sl]
        acc = cur
        for k in range(1, w):
            acc = acc + spool_ref[POOL_BUF - k, :, sl]
        cnt = float(min(PAST_LEN + 1, w))
        a = _pool_group_out(acc / cnt - cur, g, wpool_ref, pscale_ref)
        ag_ref[:, sl] = gates[:, sl] * a

    acc = cb_ref[...] + xbc * cw_ref[CONV_W - 1:CONV_W, :]
    for k in range(CONV_W - 1):
        acc = acc + sconv_ref[k] * cw_ref[k:k + 1, :]
    xc_ref[...] = _silu(acc)


def _mix_in_sample(x2d, mod_s, lw, spool_t, sconv_t):
    n = x2d.shape[0]
    full = lambda r, c: pl.BlockSpec((r, c), lambda i: (0, 0), pipeline_mode=pl.Buffered(1))
    full3 = lambda a, r, c: pl.BlockSpec((a, r, c), lambda i: (0, 0, 0), pipeline_mode=pl.Buffered(1))
    modspec = lambda k: pl.BlockSpec((n, D_MODEL), lambda i, k=k: (0, k),
                                     pipeline_mode=pl.Buffered(1))
    return pl.pallas_call(
        _mix_in_sample_kernel,
        grid=(1,),
        in_specs=[
            full(n, D_MODEL), modspec(0), modspec(1),
            _resident((1, D_MODEL)),
            _weight((D_MODEL, W_COLS)),
            _resident((D_MODEL, LANES)), _resident((1, LANES)),
            _resident((len(POOL_WINDOWS), POOL_GW, POOL_GW)), _resident((1, D_MODEL)),
            _resident((CONV_W, CONV_DIM)), _resident((1, CONV_DIM)),
            full3(POOL_BUF, n, D_MODEL), full3(CONV_W - 1, n, CONV_DIM),
        ],
        out_specs=[
            full(n, D_MODEL), full(n, D_MODEL), full(n, D_INNER), full(n, CONV_DIM), full(n, LANES),
            full3(POOL_BUF, n, D_MODEL), full3(CONV_W - 1, n, CONV_DIM),
        ],
        out_shape=[
            jax.ShapeDtypeStruct((n, D_MODEL), F32),
            jax.ShapeDtypeStruct((n, D_MODEL), F32),
            jax.ShapeDtypeStruct((n, D_INNER), F32),
            jax.ShapeDtypeStruct((n, CONV_DIM), F32),
            jax.ShapeDtypeStruct((n, LANES), F32),
            jax.ShapeDtypeStruct((POOL_BUF, n, D_MODEL), F32),
            jax.ShapeDtypeStruct((CONV_W - 1, n, CONV_DIM), F32),
        ],
        compiler_params=pltpu.CompilerParams(
            dimension_semantics=("arbitrary",), vmem_limit_bytes=VMEM_LIMIT),
        name="mix_in_sample",
    )(x2d, mod_s, mod_s, lw["n1w"], lw["w_all"], lw["wdt"], lw["dtb"],
      lw["wpool"], lw["pscale"], lw["cw"], lw["cb"], spool_t, sconv_t)


def _ssm_step_kernel(h_ref, xc_ref, dt_ref, alog_ref, dexp_ref, hout_ref, y_ref, *, bb):
    lane = lax.broadcasted_iota(jnp.int32, (bb, LANES), 1)
    head_ok = lane < HEADS
    dt = jnp.where(head_ok, dt_ref[...], 0.0)
    a_neg = jnp.where(head_ok[0:1, :], -jnp.exp(alog_ref[...]), 0.0)
    dec = jnp.exp(dt * a_neg)
    xs = xc_ref[:, 0:D_INNER]
    dt_exp = jnp.concatenate([_group_expand(dt, g) for g in range(GROUPS)], axis=1)
    pad = jnp.zeros((bb, D_INNER), F32)
    xdt = jnp.concatenate([xs * dt_exp, pad], axis=0).astype(BF16)
    bm = xc_ref[:, D_INNER:D_INNER + GROUPS * D_STATE]
    cm = xc_ref[:, D_INNER + GROUPS * D_STATE:]
    rowid = lax.broadcasted_iota(jnp.int32, (2 * bb, D_STATE), 0)
    rowid_y = lax.broadcasted_iota(jnp.int32, (2 * bb, GROUP_W), 0)
    padn = jnp.zeros((bb, D_STATE), F32)

    y_acc = [jnp.zeros((2 * bb, GROUP_W), F32) for _ in range(GROUPS)]
    for j in range(bb):
        for g in range(GROUPS):
            rs = slice(g * GROUP_W, (g + 1) * GROUP_W)
            ns = slice(g * D_STATE, (g + 1) * D_STATE)
            h_g = h_ref[j, rs, :]
            dcol = jnp.concatenate(
                [jnp.broadcast_to(dec[j:j + 1, HPG * g + r:HPG * g + r + 1], (HEADDIM, D_STATE))
                 for r in range(HPG)], axis=0)
            b16 = jnp.concatenate([bm[:, ns], padn], axis=0)
            b_j = jnp.where(rowid == j, b16, 0.0).astype(BF16)
            new = h_g * dcol + _dot_tn(xdt[:, rs], b_j)
            hout_ref[j, rs, :] = new
            c16 = jnp.concatenate([cm[:, ns], padn], axis=0).astype(BF16)
            y_all = _dot_nt(c16, new.astype(BF16))
            y_acc[g] = y_acc[g] + jnp.where(rowid_y == j, y_all, 0.0)
    y = jnp.concatenate(y_acc, axis=1)[0:bb, :]
    y_ref[...] = y + dexp_ref[...] * xs


def _ssm_step(h3, xc, dt, lw, bb=8):
    n = h3.shape[0]
    return pl.pallas_call(
        functools.partial(_ssm_step_kernel, bb=bb),
        grid=(n // bb,),
        in_specs=[
            pl.BlockSpec((bb, HEADS * HEADDIM, D_STATE), lambda i: (i, 0, 0)),
            pl.BlockSpec((bb, CONV_DIM), lambda i: (i, 0)),
            pl.BlockSpec((bb, LANES), lambda i: (i, 0)),
            _resident((1, LANES)), _resident((1, D_INNER)),
        ],
        out_specs=[
            pl.BlockSpec((bb, HEADS * HEADDIM, D_STATE), lambda i: (i, 0, 0)),
            pl.BlockSpec((bb, D_INNER), lambda i: (i, 0)),
        ],
        out_shape=[
            jax.ShapeDtypeStruct(h3.shape, F32),
            jax.ShapeDtypeStruct((n, D_INNER), F32),
        ],
        compiler_params=pltpu.CompilerParams(
            dimension_semantics=("arbitrary",), vmem_limit_bytes=VMEM_LIMIT),
        name="ssm_step",
    )(h3, xc, dt, lw["alog"], lw["dexp"])


def _mix_out_sample_kernel(y_ref, sz_ref, ag_ref, gb_ref, x_ref, g1_ref, nw_ref, wproj_ref, wout_ref,
                           o_ref):
    o_ref[...] = _mixer_tail(y_ref[...], sz_ref[...], ag_ref[...], gb_ref[...], x_ref[...],
                             g1_ref[...], nw_ref[...], wproj_ref, wout_ref)


def _mix_out_sample(y, sz, ag, gb, x2d, mod_s, lw):
    n = x2d.shape[0]
    full = lambda c: pl.BlockSpec((n, c), lambda i: (0, 0))
    return pl.pallas_call(
        _mix_out_sample_kernel,
        grid=(1,),
        in_specs=[
            full(D_INNER), full(D_INNER), full(D_MODEL), full(D_MODEL), full(D_MODEL),
            pl.BlockSpec((n, D_MODEL), lambda i: (0, 2)),
            _resident((1, D_INNER)), _weight((D_INNER, D_MODEL)), _weight((D_MODEL, D_MODEL)),
        ],
        out_specs=full(D_MODEL),
        out_shape=jax.ShapeDtypeStruct((n, D_MODEL), F32),
        compiler_params=pltpu.CompilerParams(
            dimension_semantics=("arbitrary",), vmem_limit_bytes=VMEM_LIMIT),
        name="mix_out_sample",
    )(y, sz, ag, gb, x2d, mod_s, lw["ssd_nw"], lw["wproj"], lw["wout"])


def _layer_weights(l, w_in, w_pool, pool_scale, conv_w, conv_b, dt_bias, A_log, D_skip, ssd_norm_w,
                   w_ssd_proj, w_out, norm1_w, norm2_w, w_ffn_in, w_ffn_out):
    rep = LANES // HEADS
    lane_id = jnp.arange(LANES)[:, None]
    col_head = jnp.arange(D_INNER)[None, :] // HEADDIM
    hexp = ((lane_id % HEADS == col_head) & (lane_id < 3 * HEADS)).astype(BF16)
    return dict(
        n1w=norm1_w[l].reshape(1, D_MODEL),
        **_prep_w_in(w_in, l),
        dtb=jnp.tile(dt_bias[l], rep).reshape(1, LANES),
        wpool=w_pool[l].astype(BF16),
        pscale=pool_scale[l].reshape(1, D_MODEL),
        cw=conv_w[l],
        cb=conv_b[l].reshape(1, CONV_DIM),
        alog=jnp.tile(A_log[l], rep).reshape(1, LANES),
        hexp=hexp,
        dexp=jnp.repeat(D_skip[l], HEADDIM).reshape(1, D_INNER),
        ssd_nw=ssd_norm_w[l].reshape(1, D_INNER),
        wproj=_cast_weight(w_ssd_proj, l),
        wout=_cast_weight(w_out, l),
        n2w=norm2_w[l].reshape(1, D_MODEL),
        wffn_in=_cast_weight(w_ffn_in, l, pad=0),
        wffn_out=_cast_weight(w_ffn_out, l),
    )


def kernel(x_prompt, x_sample, c_prompt, c_sample, state_pool, state_conv, state_ssm, w_ada, b_ada, norm1_w, w_in, w_pool, pool_scale, conv_w, conv_b, dt_bias, A_log, D_skip, ssd_norm_w, w_ssd_proj, w_out, norm2_w, w_ffn_in, w_ffn_out, final_norm_w):
    depth = w_in.shape[0]
    bsz, L, _ = x_prompt.shape
    nsmp = x_sample.shape[0]
    xp = x_prompt
    xs = x_sample.reshape(nsmp, D_MODEL)
    c_all = jnp.concatenate([c_prompt, c_sample], axis=0)
    fnw = final_norm_w.reshape(1, D_MODEL)
    pool_p, conv_p, ssm_p, pool_s, conv_s, ssm_s = [], [], [], [], [], []
    for l in range(depth):
        final = l == depth - 1
        lw = _layer_weights(l, w_in, w_pool, pool_scale, conv_w, conv_b, dt_bias, A_log, D_skip,
                            ssd_norm_w, w_ssd_proj, w_out, norm1_w, norm2_w, w_ffn_in, w_ffn_out)
        mod = _modulation(c_all, w_ada[l], b_ada[l])
        mod_p = mod[:bsz].reshape(bsz, 1, N_MOD * D_MODEL)
        mod_s = mod[bsz:]

        ag, gb, sz, xc, dt, pst, cst = _mix_in_prompt(xp, mod_p, lw)
        x1, hst = _ssd_prompt(xc, dt, sz, ag, gb, xp, mod_p, lw)
        tm = 256
        per = L // tm
        p_mods = (mod_p, lambda k: pl.BlockSpec((None, 1, D_MODEL), lambda i, k=k: (i // per, 0, k)))
        xp = _ffn(x1.reshape(bsz * L, D_MODEL), p_mods, lw, fnw, final, tm).reshape(bsz, L, D_MODEL)
        pool_p.append(pst)
        conv_p.append(cst)
        ssm_p.append(hst.reshape(bsz, HEADS, HEADDIM, D_STATE))

        ag, gb, sz, xc, dt, pnew_t, cnew_t = _mix_in_sample(
            xs, mod_s, lw, jnp.swapaxes(state_pool[l], 0, 1), jnp.swapaxes(state_conv[l], 0, 1))
        hnew, y = _ssm_step(state_ssm[l].reshape(nsmp, HEADS * HEADDIM, D_STATE), xc, dt, lw)
        x1s = _mix_out_sample(y, sz, ag, gb, xs, mod_s, lw)
        s_mods = (mod_s, lambda k: pl.BlockSpec((nsmp, D_MODEL), lambda i, k=k: (0, k)))
        xs = _ffn(x1s, s_mods, lw, fnw, final, nsmp)
        pool_s.append(jnp.swapaxes(pnew_t, 0, 1))
        conv_s.append(jnp.swapaxes(cnew_t, 0, 1))
        ssm_s.append(hnew.reshape(nsmp, HEADS, HEADDIM, D_STATE))

    return (xp, xs.reshape(nsmp, 1, D_MODEL), jnp.stack(pool_p), jnp.stack(conv_p), jnp.stack(ssm_p),
            jnp.stack(pool_s), jnp.stack(conv_s), jnp.stack(ssm_s))
```

```python
import functools

import jax
import jax.numpy as jnp
from jax import lax
from jax.experimental import pallas as pl
from jax.experimental.pallas import tpu as pltpu

D_MODEL = 1024
POOL_WINDOWS = (2, 4, 8, 16)
POOL_GW = D_MODEL // len(POOL_WINDOWS)
POOL_BUF = max(POOL_WINDOWS) - 1
D_INNER = 2 * D_MODEL
HEADDIM = 64
HEADS = D_INNER // HEADDIM
GROUPS = 8
HPG = HEADS // GROUPS
D_STATE = 128
CONV_W = 4
CONV_DIM = D_INNER + 2 * GROUPS * D_STATE
CHUNK = 128
D_FF = ((8 * D_MODEL // 3 + 255) // 256) * 256
N_MOD = 6
COL_Z = D_MODEL
COL_XBC = COL_Z + D_INNER
COL_DT = COL_XBC + CONV_DIM
COL_GATE = COL_DT + HEADS
IN_COLS = COL_GATE + 2 * D_MODEL
PAST_LEN = 16384
EPS = 1e-6

LANES = 128
SUBLANES = 8
MXU_N = 256
WPAD = LANES
GROUP_W = HPG * HEADDIM
POOL_HALO = 16
CONV_HALO = 8
CONV_ROWS = 64
VMEM_LIMIT = 60 * 1024 * 1024

BF16 = jnp.bfloat16
F32 = jnp.float32


def _dot(a, b):
    return jnp.dot(a, b, preferred_element_type=F32)


def _dot_nt(a, b):
    return lax.dot_general(a, b, (((1,), (1,)), ((), ())), preferred_element_type=F32)


def _dot_tn(a, b):
    return lax.dot_general(a, b, (((0,), (0,)), ((), ())), preferred_element_type=F32)


def _silu_of_half(h):
    return h * jnp.tanh(h) + h


def _silu(x):
    return _silu_of_half(0.5 * x)


def _rms(x, w):
    return x * lax.rsqrt(jnp.mean(x * x, axis=-1, keepdims=True) + EPS) * w


def _softplus(x):
    return jnp.maximum(x, 0.0) + jnp.log1p(jnp.exp(-jnp.abs(x)))


def _resident(shape):
    nd = len(shape)
    return pl.BlockSpec(shape, lambda *_: (0,) * nd, pipeline_mode=pl.Buffered(1))


def _weight(shape):
    return _resident((shape[0], shape[1] + WPAD))


def _cast_weight_kernel(w_ref, o_ref):
    n = w_ref.shape[1]
    o_ref[:, 0:n] = w_ref[...].astype(BF16)
    if o_ref.shape[1] > n:
        o_ref[:, n:] = jnp.zeros((o_ref.shape[0], o_ref.shape[1] - n), BF16)


def _cast_weight(w, l, pad=WPAD, steps=4):
    _, k, n = w.shape
    rows = k // steps
    return pl.pallas_call(
        _cast_weight_kernel,
        grid=(k // rows,),
        in_specs=[pl.BlockSpec((None, rows, n), lambda i: (l, i, 0))],
        out_specs=pl.BlockSpec((rows, n + pad), lambda i: (i, 0)),
        out_shape=jax.ShapeDtypeStruct((k, n + pad), BF16),
        compiler_params=pltpu.CompilerParams(dimension_semantics=("arbitrary",)),
        name="cast_weight",
    )(w)


def _mod_kernel(c_ref, w_ref, b_ref, o_ref):
    a = _silu(c_ref[...]).astype(BF16)
    o_ref[...] = _dot(a, w_ref[...].astype(BF16)) + b_ref[...]


def _modulation(c, w_ada, b_ada):
    m = c.shape[0]
    n = w_ada.shape[1]
    tn = 1024
    return pl.pallas_call(
        _mod_kernel,
        grid=(n // tn,),
        in_specs=[
            pl.BlockSpec((m, D_MODEL), lambda j: (0, 0)),
            pl.BlockSpec((D_MODEL, tn), lambda j: (0, j)),
            pl.BlockSpec((1, tn), lambda j: (0, j)),
        ],
        out_specs=pl.BlockSpec((m, tn), lambda j: (0, j)),
        out_shape=jax.ShapeDtypeStruct((m, n), F32),
        name="adaln_mod",
    )(c, w_ada, b_ada.reshape(1, n))


WCOL_G = COL_DT
W_COLS = WCOL_G + 2 * D_MODEL
PREP_ROWS = 512
W_ALL_COLS = W_COLS + PREP_ROWS


def _prep_w_in_kernel(wt_ref, wt_dt_ref, w_ref, wdt_ref):
    i = pl.program_id(0)
    last = pl.num_programs(0) - 1
    halve = (i >= COL_Z // PREP_ROWS) & (i < COL_XBC // PREP_ROWS)
    scale = jnp.where(halve, 0.5, 1.0)

    @pl.when(i < last)
    def _():
        w_ref[...] = (scale * wt_ref[0]).astype(BF16).T

    @pl.when(i == last)
    def _():
        w_ref[...] = jnp.zeros_like(w_ref)

    @pl.when(i == 0)
    def _():
        d = wt_dt_ref[0].T[:, 0:HEADS]
        wdt_ref[...] = jnp.concatenate([d] * (LANES // HEADS), axis=1).astype(BF16)


def _prep_w_in(w_in, l):
    wt = jnp.swapaxes(w_in, 1, 2)
    n_main = COL_DT // PREP_ROWS
    n_blocks = W_COLS // PREP_ROWS
    elem_block = (pl.Element(1), pl.Element(PREP_ROWS), pl.Element(D_MODEL))
    dt_block = (pl.Element(1), pl.Element(LANES), pl.Element(D_MODEL))

    def src_row(i):
        j = jnp.minimum(i, n_blocks - 1)
        per = PREP_ROWS // SUBLANES
        tile_row = jnp.where(j < n_main, j * per, COL_GATE // SUBLANES + (j - n_main) * per)
        return tile_row * SUBLANES

    w_all, wdt = pl.pallas_call(
        _prep_w_in_kernel,
        grid=(W_ALL_COLS // PREP_ROWS,),
        in_specs=[
            pl.BlockSpec(elem_block, lambda i: (l, src_row(i), 0)),
            pl.BlockSpec(dt_block, lambda i: (l, COL_DT, 0)),
        ],
        out_specs=[
            pl.BlockSpec((D_MODEL, PREP_ROWS), lambda i: (0, i)),
            pl.BlockSpec((D_MODEL, LANES), lambda i: (0, 0)),
        ],
        out_shape=[
            jax.ShapeDtypeStruct((D_MODEL, W_ALL_COLS), BF16),
            jax.ShapeDtypeStruct((D_MODEL, LANES), BF16),
        ],
        compiler_params=pltpu.CompilerParams(dimension_semantics=("arbitrary",)),
        name="prep_w_in",
    )(wt, wt)
    return dict(w_all=w_all, wdt=wdt)


def _norm_mod(x, sh, sc, w):
    return (_rms(x, w) * (1.0 + sc) + sh).astype(BF16)


def _pool_group_out(pooled, g, wpool_ref, pscale_ref):
    sl = slice(g * POOL_GW, (g + 1) * POOL_GW)
    return _dot(pooled.astype(BF16), wpool_ref[g]) * pscale_ref[:, sl]


def _mix_in_prompt_kernel(x_ref, sh_ref, sc_ref, n1w_ref, w_ref, wdt_ref,
                          dtb_ref, wpool_ref, pscale_ref, cw_ref, cb_ref,
                          ag_ref, gb_ref, sz_ref, xc_ref, dt_ref, pst_ref, cst_ref,
                          ubuf, xbuf, *, tm):
    i = pl.program_id(1)

    @pl.when(i == 0)
    def _():
        ubuf[0:POOL_HALO, :] = jnp.zeros((POOL_HALO, D_MODEL), F32)
        xbuf[0:CONV_HALO, :] = jnp.zeros((CONV_HALO, CONV_DIM), F32)

    @pl.when(i > 0)
    def _():
        ubuf[0:POOL_HALO, :] = ubuf[tm:tm + POOL_HALO, :]
        xbuf[0:CONV_HALO, :] = xbuf[tm:tm + CONV_HALO, :]

    hb = _norm_mod(x_ref[...], sh_ref[...], sc_ref[...], n1w_ref[...])
    dt_ref[...] = _softplus(_dot(hb, wdt_ref[...]) + dtb_ref[...])

    pos = i * tm + lax.broadcasted_iota(jnp.int32, (tm, POOL_GW), 0)

    def pool_group(g):
        w = POOL_WINDOWS[g]
        sl = slice(g * POOL_GW, (g + 1) * POOL_GW)
        ubuf[POOL_HALO:POOL_HALO + tm, sl] = _dot(hb, w_ref[:, sl])
        full = ubuf[:, sl]
        acc = full
        span = 1
        while span < w:
            acc = acc + pltpu.roll(acc, span, axis=0)
            span *= 2
        cur = full[POOL_HALO:, :]
        cnt = jnp.minimum(pos + 1, w).astype(F32)
        a = _pool_group_out(acc[POOL_HALO:, :] / cnt - cur, g, wpool_ref, pscale_ref)
        gate_a = _dot(hb, w_ref[:, WCOL_G + g * POOL_GW:WCOL_G + (g + 1) * POOL_GW])
        ag_ref[:, sl] = jax.nn.sigmoid(gate_a) * a

    cchunk = 1024

    def gate_b_chunk(c0):
        gb_ref[:, c0:c0 + cchunk] = jax.nn.sigmoid(
            _dot(hb, w_ref[:, WCOL_G + D_MODEL + c0:WCOL_G + D_MODEL + c0 + cchunk]))

    def z_chunk(c0):
        sz_ref[:, c0:c0 + cchunk] = _silu_of_half(_dot(hb, w_ref[:, COL_Z + c0:COL_Z + c0 + cchunk]))

    def conv_chunk(c0):
        for n0 in range(c0, c0 + cchunk, MXU_N):
            xbuf[CONV_HALO:CONV_HALO + tm, n0:n0 + MXU_N] = _dot(
                hb, w_ref[:, COL_XBC + n0:COL_XBC + n0 + MXU_N])
        for l0 in range(c0, c0 + cchunk, LANES):
            sl = slice(l0, l0 + LANES)
            taps = 0.5 * cw_ref[:, sl]
            bias = 0.5 * cb_ref[:, sl]
            for r0 in range(0, tm, CONV_ROWS):
                src = xbuf[r0:r0 + CONV_HALO + CONV_ROWS, sl]
                acc = bias + src[CONV_HALO:, :] * taps[CONV_W - 1:CONV_W, :]
                for k in range(CONV_W - 1):
                    back = CONV_W - 1 - k
                    acc = acc + pltpu.roll(src, back, axis=0)[CONV_HALO:, :] * taps[k:k + 1, :]
                xc_ref[r0:r0 + CONV_ROWS, sl] = _silu_of_half(acc)

    light = ([functools.partial(pool_group, g) for g in range(len(POOL_WINDOWS))]
             + [functools.partial(gate_b_chunk, c0) for c0 in range(0, D_MODEL, cchunk)]
             + [functools.partial(z_chunk, c0) for c0 in range(0, D_INNER, cchunk)])
    heavy = [functools.partial(conv_chunk, c0) for c0 in range(0, CONV_DIM, cchunk)]
    for k in range(max(len(light), len(heavy))):
        if k < len(light):
            light[k]()
        if k < len(heavy):
            heavy[k]()

    @pl.when(i == pl.num_programs(1) - 1)
    def _():
        pst_ref[...] = ubuf[POOL_HALO + tm - POOL_BUF:POOL_HALO + tm, :]
        cst_ref[...] = xbuf[CONV_HALO + tm - (CONV_W - 1):CONV_HALO + tm, :]


def _mix_in_prompt(x, mod3, lw, tm=256):
    bsz, L, _ = x.shape
    nt = L // tm
    tok = lambda w: pl.BlockSpec((None, tm, w), lambda b, i: (b, i, 0))
    modspec = lambda k: pl.BlockSpec((None, 1, D_MODEL), lambda b, i, k=k: (b, 0, k))
    outs = pl.pallas_call(
        functools.partial(_mix_in_prompt_kernel, tm=tm),
        grid=(bsz, nt),
        in_specs=[
            tok(D_MODEL), modspec(0), modspec(1),
            _resident((1, D_MODEL)),
            _resident((D_MODEL, W_ALL_COLS)),
            _resident((D_MODEL, LANES)), _resident((1, LANES)),
            _resident((len(POOL_WINDOWS), POOL_GW, POOL_GW)), _resident((1, D_MODEL)),
            _resident((CONV_W, CONV_DIM)), _resident((1, CONV_DIM)),
        ],
        out_specs=[
            tok(D_MODEL), tok(D_MODEL), tok(D_INNER), tok(CONV_DIM), tok(LANES),
            pl.BlockSpec((None, POOL_BUF, D_MODEL), lambda b, i: (b, 0, 0)),
            pl.BlockSpec((None, CONV_W - 1, CONV_DIM), lambda b, i: (b, 0, 0)),
        ],
        out_shape=[
            jax.ShapeDtypeStruct((bsz, L, D_MODEL), F32),
            jax.ShapeDtypeStruct((bsz, L, D_MODEL), F32),
            jax.ShapeDtypeStruct((bsz, L, D_INNER), F32),
            jax.ShapeDtypeStruct((bsz, L, CONV_DIM), F32),
            jax.ShapeDtypeStruct((bsz, L, LANES), F32),
            jax.ShapeDtypeStruct((bsz, POOL_BUF, D_MODEL), F32),
            jax.ShapeDtypeStruct((bsz, CONV_W - 1, CONV_DIM), F32),
        ],
        scratch_shapes=[
            pltpu.VMEM((POOL_HALO + tm, D_MODEL), F32),
            pltpu.VMEM((CONV_HALO + tm, CONV_DIM), F32),
        ],
        compiler_params=pltpu.CompilerParams(
            dimension_semantics=("arbitrary", "arbitrary"), vmem_limit_bytes=VMEM_LIMIT),
        name="mix_in_prompt",
    )(x, mod3, mod3, lw["n1w"], lw["w_all"], lw["wdt"], lw["dtb"],
      lw["wpool"], lw["pscale"], lw["cw"], lw["cb"])
    return outs


def _mixer_tail(y, sz, ag, gb, x, g1, nw, wproj_ref, wout_ref):
    yn = _rms(y * sz, nw).astype(BF16)
    b_out = _dot(yn, wproj_ref[:, 0:D_MODEL])
    mix = (ag + gb * b_out).astype(BF16)
    return x + g1 * _dot(mix, wout_ref[:, 0:D_MODEL])


def _head_pair_expand(v, h0):
    rows = v.shape[0]
    lane = lax.broadcasted_iota(jnp.int32, (rows, LANES), 1)
    lo = jnp.broadcast_to(v[:, h0:h0 + 1], (rows, LANES))
    hi = jnp.broadcast_to(v[:, h0 + 1:h0 + 2], (rows, LANES))
    return jnp.where(lane < HEADDIM, lo, hi)


def _group_expand(v, g):
    return jnp.concatenate(
        [_head_pair_expand(v, HPG * g + 2 * t) for t in range(GROUP_W // LANES)], axis=1)


def _ssd_prompt_kernel(xc_ref, dt_ref, sz_ref, ag_ref, gb_ref, x_ref, g1_ref, alog_ref, dexp_ref,
                       hexp_ref, nw_ref, wproj_ref, wout_ref,
                       x1_ref, hout_ref, ht_scr, y_scr, *, rows):
    step = pl.program_id(1)

    @pl.when(step == 0)
    def _():
        ht_scr[...] = jnp.zeros_like(ht_scr)

    q = CHUNK
    row = lax.broadcasted_iota(jnp.int32, (q, q), 0)
    lane = lax.broadcasted_iota(jnp.int32, (q, q), 1)
    causal = row >= lane
    tri = causal.astype(BF16)
    tri3 = jnp.concatenate([tri, tri, tri], axis=1)
    a_neg = -jnp.exp(alog_ref[...])
    lane_g = lax.broadcasted_iota(jnp.int32, (q, GROUP_W), 1)

    for r0 in range(0, rows, q):
        rs = slice(r0, r0 + q)
        dt = dt_ref[rs, :]
        d_a = dt * a_neg
        a_hi = d_a.astype(BF16)
        a_r1 = d_a - a_hi.astype(F32)
        a_mid = a_r1.astype(BF16)
        a_lo = (a_r1 - a_mid.astype(F32)).astype(BF16)
        acs = _dot(tri3, jnp.concatenate([a_hi, a_mid, a_lo], axis=0))
        acs_t = acs.T
        dt_t = dt.T
        w_t = dt_t * jnp.exp(acs_t[:, q - 1:q] - acs_t)

        e_acs = jnp.exp(acs)
        e_hi = e_acs.astype(BF16).astype(F32)
        e_r1 = e_acs - e_hi
        e_mid = e_r1.astype(BF16).astype(F32)
        e_lo = e_r1 - e_mid
        pieces = jnp.where(lane < HEADS, e_hi,
                           jnp.where(lane < 2 * HEADS, e_mid,
                                     jnp.where(lane < 3 * HEADS, e_lo, 0.0))).astype(BF16)
        e_exp = _dot(pieces, hexp_ref[...])

        c_bf, bt_f, cbs = [], [], []
        for g in range(GROUPS):
            b0 = D_INNER + g * D_STATE
            c0 = D_INNER + GROUPS * D_STATE + g * D_STATE
            bt = xc_ref[rs, b0:b0 + D_STATE].T
            cg = xc_ref[rs, c0:c0 + D_STATE].astype(BF16)
            c_bf.append(cg)
            bt_f.append(bt)
            cbs.append(_dot(cg, bt.astype(BF16)))

        for g in range(GROUPS):
            gs = slice(g * GROUP_W, (g + 1) * GROUP_W)
            x_g = xc_ref[rs, gs]
            x_bf = x_g.astype(BF16)
            scores, btw, blocks = [], [], []
            for r in range(HPG):
                h = HPG * g + r
                seg = acs[:, h:h + 1] - acs_t[h:h + 1, :]
                decay = jnp.exp(jnp.where(causal, seg, -jnp.inf))
                scores.append((cbs[g] * decay * dt_t[h:h + 1, :]).astype(BF16))
                btw.append((bt_f[g] * w_t[h:h + 1, :]).astype(BF16))
                in_head = (lane_g >= r * HEADDIM) & (lane_g < (r + 1) * HEADDIM)
                blocks.append(jnp.where(in_head, x_bf, jnp.zeros_like(x_bf)))
            lhs = jnp.concatenate(
                [jnp.concatenate(scores, axis=1), jnp.concatenate(btw, axis=1)], axis=0)
            both = _dot(lhs, jnp.concatenate(blocks, axis=0))
            ht_prev = ht_scr[g * D_STATE:(g + 1) * D_STATE, :]
            y_off = _dot(c_bf[g], ht_prev.astype(BF16)) * e_exp[:, gs]
            y_scr[rs, gs] = both[0:q, :] + y_off + dexp_ref[:, gs] * x_g
            ht_scr[g * D_STATE:(g + 1) * D_STATE, :] = (
                ht_prev * e_exp[q - 1:q, gs] + both[q:2 * q, :])

    x1_ref[...] = _mixer_tail(y_scr[...], sz_ref[...], ag_ref[...], gb_ref[...], x_ref[...],
                              g1_ref[...], nw_ref[...], wproj_ref, wout_ref)

    @pl.when(step == pl.num_programs(1) - 1)
    def _():
        for g in range(GROUPS):
            hout_ref[g * GROUP_W:(g + 1) * GROUP_W, :] = ht_scr[g * D_STATE:(g + 1) * D_STATE, :].T


def _ssd_prompt(xc, dt, sz, ag, gb, x, mod3, lw, rows=4 * CHUNK):
    bsz, L, _ = x.shape
    tok = lambda w: pl.BlockSpec((None, rows, w), lambda b, c: (b, c, 0))
    return pl.pallas_call(
        functools.partial(_ssd_prompt_kernel, rows=rows),
        grid=(bsz, L // rows),
        in_specs=[
            tok(CONV_DIM), tok(LANES), tok(D_INNER), tok(D_MODEL), tok(D_MODEL), tok(D_MODEL),
            pl.BlockSpec((None, 1, D_MODEL), lambda b, c: (b, 0, 2)),
            _resident((1, LANES)), _resident((1, D_INNER)), _resident((LANES, D_INNER)),
            _resident((1, D_INNER)),
            _weight((D_INNER, D_MODEL)), _weight((D_MODEL, D_MODEL)),
        ],
        out_specs=[
            tok(D_MODEL),
            pl.BlockSpec((None, HEADS * HEADDIM, D_STATE), lambda b, c: (b, 0, 0)),
        ],
        out_shape=[
            jax.ShapeDtypeStruct((bsz, L, D_MODEL), F32),
            jax.ShapeDtypeStruct((bsz, HEADS * HEADDIM, D_STATE), F32),
        ],
        scratch_shapes=[
            pltpu.VMEM((GROUPS * D_STATE, GROUP_W), F32),
            pltpu.VMEM((rows, D_INNER), F32),
        ],
        compiler_params=pltpu.CompilerParams(
            dimension_semantics=("arbitrary", "arbitrary"), vmem_limit_bytes=VMEM_LIMIT),
        name="ssd_prompt",
    )(xc, dt, sz, ag, gb, x, mod3, lw["alog"], lw["dexp"], lw["hexp"], lw["ssd_nw"], lw["wproj"],
      lw["wout"])


def _ffn_kernel(x_ref, sh_ref, sc_ref, g2_ref, n2w_ref, win_ref, wout_ref, fnw_ref, o_ref, act_scr,
                *, final):
    x = x_ref[...]
    hb = _norm_mod(x, sh_ref[...], sc_ref[...], n2w_ref[...])
    cw = 256
    for c0 in range(0, D_FF, cw):
        gt = _dot(hb, win_ref[:, c0:c0 + cw])
        up = _dot(hb, win_ref[:, D_FF + c0:D_FF + c0 + cw])
        act_scr[:, c0:c0 + cw] = (_silu(gt) * up).astype(BF16)
    x2 = x + g2_ref[...] * _dot(act_scr[...], wout_ref[:, 0:D_MODEL])
    o_ref[...] = _rms(x2, fnw_ref[...]) if final else x2


def _ffn(x2d, mods, lw, fnw, final, tm):
    t = x2d.shape[0]
    mod_arr, modspec = mods
    return pl.pallas_call(
        functools.partial(_ffn_kernel, final=final),
        grid=(t // tm,),
        in_specs=[
            pl.BlockSpec((tm, D_MODEL), lambda i: (i, 0)),
            modspec(3), modspec(4), modspec(5),
            _resident((1, D_MODEL)),
            _resident((D_MODEL, 2 * D_FF)), _weight((D_FF, D_MODEL)),
            _resident((1, D_MODEL)),
        ],
        out_specs=pl.BlockSpec((tm, D_MODEL), lambda i: (i, 0)),
        out_shape=jax.ShapeDtypeStruct((t, D_MODEL), F32),
        scratch_shapes=[pltpu.VMEM((tm, D_FF), BF16)],
        compiler_params=pltpu.CompilerParams(
            dimension_semantics=("arbitrary",), vmem_limit_bytes=VMEM_LIMIT),
        name="ffn",
    )(x2d, mod_arr, mod_arr, mod_arr, lw["n2w"], lw["wffn_in"], lw["wffn_out"], fnw)


def _mix_in_sample_kernel(x_ref, sh_ref, sc_ref, n1w_ref, w_ref, wdt_ref,
                          dtb_ref, wpool_ref, pscale_ref, cw_ref, cb_ref, spool_ref, sconv_ref,
                          ag_ref, gb_ref, sz_ref, xc_ref, dt_ref, pnew_ref, cnew_ref):
    hb = _norm_mod(x_ref[...], sh_ref[...], sc_ref[...], n1w_ref[...])
    u = _dot(hb, w_ref[:, 0:COL_Z])
    pnew_ref[0:POOL_BUF - 1] = spool_ref[1:POOL_BUF]
    pnew_ref[POOL_BUF - 1] = u
    sz_ref[...] = _silu_of_half(_dot(hb, w_ref[:, COL_Z:COL_XBC]))
    xbc = _dot(hb, w_ref[:, COL_XBC:COL_DT])
    cnew_ref[0:CONV_W - 2] = sconv_ref[1:CONV_W - 1]
    cnew_ref[CONV_W - 2] = xbc
    gates = jax.nn.sigmoid(_dot(hb, w_ref[:, WCOL_G:W_COLS]))
    gb_ref[...] = gates[:, D_MODEL:]
    dt_ref[...] = _softplus(_dot(hb, wdt_ref[...]) + dtb_ref[...])

    for g, w in enumerate(POOL_WINDOWS):
        sl = slice(g * POOL_GW, (g + 1) * POOL_GW)
        cur = u[:, sl]
        acc = cur
        for k in range(1, w):
            acc = acc + spool_ref[POOL_BUF - k, :, sl]
        cnt = float(min(PAST_LEN + 1, w))
        a = _pool_group_out(acc / cnt - cur, g, wpool_ref, pscale_ref)
        ag_ref[:, sl] = gates[:, sl] * a

    acc = cb_ref[...] + xbc * cw_ref[CONV_W - 1:CONV_W, :]
    for k in range(CONV_W - 1):
        acc = acc + sconv_ref[k] * cw_ref[k:k + 1, :]
    xc_ref[...] = _silu(acc)


def _mix_in_sample(x2d, mod_s, lw, spool_t, sconv_t):
    n = x2d.shape[0]
    full = lambda r, c: pl.BlockSpec((r, c), lambda i: (0, 0), pipeline_mode=pl.Buffered(1))
    full3 = lambda a, r, c: pl.BlockSpec((a, r, c), lambda i: (0, 0, 0), pipeline_mode=pl.Buffered(1))
    modspec = lambda k: pl.BlockSpec((n, D_MODEL), lambda i, k=k: (0, k),
                                     pipeline_mode=pl.Buffered(1))
    return pl.pallas_call(
        _mix_in_sample_kernel,
        grid=(1,),
        in_specs=[
            full(n, D_MODEL), modspec(0), modspec(1),
            _resident((1, D_MODEL)),
            _resident((D_MODEL, W_ALL_COLS)),
            _resident((D_MODEL, LANES)), _resident((1, LANES)),
            _resident((len(POOL_WINDOWS), POOL_GW, POOL_GW)), _resident((1, D_MODEL)),
            _resident((CONV_W, CONV_DIM)), _resident((1, CONV_DIM)),
            full3(POOL_BUF, n, D_MODEL), full3(CONV_W - 1, n, CONV_DIM),
        ],
        out_specs=[
            full(n, D_MODEL), full(n, D_MODEL), full(n, D_INNER), full(n, CONV_DIM), full(n, LANES),
            full3(POOL_BUF, n, D_MODEL), full3(CONV_W - 1, n, CONV_DIM),
        ],
        out_shape=[
            jax.ShapeDtypeStruct((n, D_MODEL), F32),
            jax.ShapeDtypeStruct((n, D_MODEL), F32),
            jax.ShapeDtypeStruct((n, D_INNER), F32),
            jax.ShapeDtypeStruct((n, CONV_DIM), F32),
            jax.ShapeDtypeStruct((n, LANES), F32),
            jax.ShapeDtypeStruct((POOL_BUF, n, D_MODEL), F32),
            jax.ShapeDtypeStruct((CONV_W - 1, n, CONV_DIM), F32),
        ],
        compiler_params=pltpu.CompilerParams(
            dimension_semantics=("arbitrary",), vmem_limit_bytes=VMEM_LIMIT),
        name="mix_in_sample",
    )(x2d, mod_s, mod_s, lw["n1w"], lw["w_all"], lw["wdt"], lw["dtb"],
      lw["wpool"], lw["pscale"], lw["cw"], lw["cb"], spool_t, sconv_t)


def _ssm_step_kernel(h_ref, xc_ref, dt_ref, alog_ref, dexp_ref, hout_ref, y_ref, *, bb):
    lane = lax.broadcasted_iota(jnp.int32, (bb, LANES), 1)
    head_ok = lane < HEADS
    dt = jnp.where(head_ok, dt_ref[...], 0.0)
    a_neg = jnp.where(head_ok[0:1, :], -jnp.exp(alog_ref[...]), 0.0)
    dec = jnp.exp(dt * a_neg)
    xs = xc_ref[:, 0:D_INNER]
    dt_exp = jnp.concatenate([_group_expand(dt, g) for g in range(GROUPS)], axis=1)
    pad = jnp.zeros((bb, D_INNER), F32)
    xdt = jnp.concatenate([xs * dt_exp, pad], axis=0).astype(BF16)
    bm = xc_ref[:, D_INNER:D_INNER + GROUPS * D_STATE]
    cm = xc_ref[:, D_INNER + GROUPS * D_STATE:]
    rowid = lax.broadcasted_iota(jnp.int32, (2 * bb, D_STATE), 0)
    rowid_y = lax.broadcasted_iota(jnp.int32, (2 * bb, GROUP_W), 0)
    padn = jnp.zeros((bb, D_STATE), F32)

    y_acc = [jnp.zeros((2 * bb, GROUP_W), F32) for _ in range(GROUPS)]
    for j in range(bb):
        for g in range(GROUPS):
            rs = slice(g * GROUP_W, (g + 1) * GROUP_W)
            ns = slice(g * D_STATE, (g + 1) * D_STATE)
            h_g = h_ref[j, rs, :]
            dcol = jnp.concatenate(
                [jnp.broadcast_to(dec[j:j + 1, HPG * g + r:HPG * g + r + 1], (HEADDIM, D_STATE))
                 for r in range(HPG)], axis=0)
            b16 = jnp.concatenate([bm[:, ns], padn], axis=0)
            b_j = jnp.where(rowid == j, b16, 0.0).astype(BF16)
            new = h_g * dcol + _dot_tn(xdt[:, rs], b_j)
            hout_ref[j, rs, :] = new
            c16 = jnp.concatenate([cm[:, ns], padn], axis=0).astype(BF16)
            y_all = _dot_nt(c16, new.astype(BF16))
            y_acc[g] = y_acc[g] + jnp.where(rowid_y == j, y_all, 0.0)
    y = jnp.concatenate(y_acc, axis=1)[0:bb, :]
    y_ref[...] = y + dexp_ref[...] * xs


def _ssm_step(h3, xc, dt, lw, bb=8):
    n = h3.shape[0]
    return pl.pallas_call(
        functools.partial(_ssm_step_kernel, bb=bb),
        grid=(n // bb,),
        in_specs=[
            pl.BlockSpec((bb, HEADS * HEADDIM, D_STATE), lambda i: (i, 0, 0)),
            pl.BlockSpec((bb, CONV_DIM), lambda i: (i, 0)),
            pl.BlockSpec((bb, LANES), lambda i: (i, 0)),
            _resident((1, LANES)), _resident((1, D_INNER)),
        ],
        out_specs=[
            pl.BlockSpec((bb, HEADS * HEADDIM, D_STATE), lambda i: (i, 0, 0)),
            pl.BlockSpec((bb, D_INNER), lambda i: (i, 0)),
        ],
        out_shape=[
            jax.ShapeDtypeStruct(h3.shape, F32),
            jax.ShapeDtypeStruct((n, D_INNER), F32),
        ],
        compiler_params=pltpu.CompilerParams(
            dimension_semantics=("arbitrary",), vmem_limit_bytes=VMEM_LIMIT),
        name="ssm_step",
    )(h3, xc, dt, lw["alog"], lw["dexp"])


def _mix_out_sample_kernel(y_ref, sz_ref, ag_ref, gb_ref, x_ref, g1_ref, nw_ref, wproj_ref, wout_ref,
                           o_ref):
    o_ref[...] = _mixer_tail(y_ref[...], sz_ref[...], ag_ref[...], gb_ref[...], x_ref[...],
                             g1_ref[...], nw_ref[...], wproj_ref, wout_ref)


def _mix_out_sample(y, sz, ag, gb, x2d, mod_s, lw):
    n = x2d.shape[0]
    full = lambda c: pl.BlockSpec((n, c), lambda i: (0, 0))
    return pl.pallas_call(
        _mix_out_sample_kernel,
        grid=(1,),
        in_specs=[
            full(D_INNER), full(D_INNER), full(D_MODEL), full(D_MODEL), full(D_MODEL),
            pl.BlockSpec((n, D_MODEL), lambda i: (0, 2)),
            _resident((1, D_INNER)), _weight((D_INNER, D_MODEL)), _weight((D_MODEL, D_MODEL)),
        ],
        out_specs=full(D_MODEL),
        out_shape=jax.ShapeDtypeStruct((n, D_MODEL), F32),
        compiler_params=pltpu.CompilerParams(
            dimension_semantics=("arbitrary",), vmem_limit_bytes=VMEM_LIMIT),
        name="mix_out_sample",
    )(y, sz, ag, gb, x2d, mod_s, lw["ssd_nw"], lw["wproj"], lw["wout"])


def _layer_weights(l, w_in, w_pool, pool_scale, conv_w, conv_b, dt_bias, A_log, D_skip, ssd_norm_w,
                   w_ssd_proj, w_out, norm1_w, norm2_w, w_ffn_in, w_ffn_out):
    rep = LANES // HEADS
    lane_id = jnp.arange(LANES)[:, None]
    col_head = jnp.arange(D_INNER)[None, :] // HEADDIM
    hexp = ((lane_id % HEADS == col_head) & (lane_id < 3 * HEADS)).astype(BF16)
    return dict(
        n1w=norm1_w[l].reshape(1, D_MODEL),
        **_prep_w_in(w_in, l),
        dtb=jnp.tile(dt_bias[l], rep).reshape(1, LANES),
        wpool=w_pool[l].astype(BF16),
        pscale=pool_scale[l].reshape(1, D_MODEL),
        cw=conv_w[l],
        cb=conv_b[l].reshape(1, CONV_DIM),
        alog=jnp.tile(A_log[l], rep).reshape(1, LANES),
        hexp=hexp,
        dexp=jnp.repeat(D_skip[l], HEADDIM).reshape(1, D_INNER),
        ssd_nw=ssd_norm_w[l].reshape(1, D_INNER),
        wproj=_cast_weight(w_ssd_proj, l),
        wout=_cast_weight(w_out, l),
        n2w=norm2_w[l].reshape(1, D_MODEL),
        wffn_in=_cast_weight(w_ffn_in, l, pad=0),
        wffn_out=_cast_weight(w_ffn_out, l),
    )


def kernel(x_prompt, x_sample, c_prompt, c_sample, state_pool, state_conv, state_ssm, w_ada, b_ada, norm1_w, w_in, w_pool, pool_scale, conv_w, conv_b, dt_bias, A_log, D_skip, ssd_norm_w, w_ssd_proj, w_out, norm2_w, w_ffn_in, w_ffn_out, final_norm_w):
    depth = w_in.shape[0]
    bsz, L, _ = x_prompt.shape
    nsmp = x_sample.shape[0]
    xp = x_prompt
    xs = x_sample.reshape(nsmp, D_MODEL)
    c_all = jnp.concatenate([c_prompt, c_sample], axis=0)
    fnw = final_norm_w.reshape(1, D_MODEL)
    pool_p, conv_p, ssm_p, pool_s, conv_s, ssm_s = [], [], [], [], [], []
    for l in range(depth):
        final = l == depth - 1
        lw = _layer_weights(l, w_in, w_pool, pool_scale, conv_w, conv_b, dt_bias, A_log, D_skip,
                            ssd_norm_w, w_ssd_proj, w_out, norm1_w, norm2_w, w_ffn_in, w_ffn_out)
        mod = _modulation(c_all, w_ada[l], b_ada[l])
        mod_p = mod[:bsz].reshape(bsz, 1, N_MOD * D_MODEL)
        mod_s = mod[bsz:]

        ag, gb, sz, xc, dt, pst, cst = _mix_in_prompt(xp, mod_p, lw)
        x1, hst = _ssd_prompt(xc, dt, sz, ag, gb, xp, mod_p, lw)
        tm = 1024
        per = L // tm
        p_mods = (mod_p, lambda k: pl.BlockSpec((None, 1, D_MODEL), lambda i, k=k: (i // per, 0, k)))
        xp = _ffn(x1.reshape(bsz * L, D_MODEL), p_mods, lw, fnw, final, tm).reshape(bsz, L, D_MODEL)
        pool_p.append(pst)
        conv_p.append(cst)
        ssm_p.append(hst.reshape(bsz, HEADS, HEADDIM, D_STATE))

        ag, gb, sz, xc, dt, pnew_t, cnew_t = _mix_in_sample(
            xs, mod_s, lw, jnp.swapaxes(state_pool[l], 0, 1), jnp.swapaxes(state_conv[l], 0, 1))
        hnew, y = _ssm_step(state_ssm[l].reshape(nsmp, HEADS * HEADDIM, D_STATE), xc, dt, lw)
        x1s = _mix_out_sample(y, sz, ag, gb, xs, mod_s, lw)
        s_mods = (mod_s, lambda k: pl.BlockSpec((nsmp, D_MODEL), lambda i, k=k: (0, k)))
        xs = _ffn(x1s, s_mods, lw, fnw, final, nsmp)
        pool_s.append(jnp.swapaxes(pnew_t, 0, 1))
        conv_s.append(jnp.swapaxes(cnew_t, 0, 1))
        ssm_s.append(hnew.reshape(nsmp, HEADS, HEADDIM, D_STATE))

    return (xp, xs.reshape(nsmp, 1, D_MODEL), jnp.stack(pool_p), jnp.stack(conv_p), jnp.stack(ssm_p),
            jnp.stack(pool_s), jnp.stack(conv_s), jnp.stack(ssm_s))
```

```python
import functools

import jax
import jax.numpy as jnp
from jax import lax
from jax.experimental import pallas as pl
from jax.experimental.pallas import tpu as pltpu

D_MODEL = 1024
POOL_WINDOWS = (2, 4, 8, 16)
POOL_GW = D_MODEL // len(POOL_WINDOWS)
POOL_BUF = max(POOL_WINDOWS) - 1
D_INNER = 2 * D_MODEL
HEADDIM = 64
HEADS = D_INNER // HEADDIM
GROUPS = 8
HPG = HEADS // GROUPS
D_STATE = 128
CONV_W = 4
CONV_DIM = D_INNER + 2 * GROUPS * D_STATE
CHUNK = 128
D_FF = ((8 * D_MODEL // 3 + 255) // 256) * 256
N_MOD = 6
COL_Z = D_MODEL
COL_XBC = COL_Z + D_INNER
COL_DT = COL_XBC + CONV_DIM
COL_GATE = COL_DT + HEADS
IN_COLS = COL_GATE + 2 * D_MODEL
PAST_LEN = 16384
EPS = 1e-6
LOG2_E = 1.4426950408889634

LANES = 128
SUBLANES = 8
MXU_N = 256
WPAD = LANES
GROUP_W = HPG * HEADDIM
POOL_HALO = 16
CONV_HALO = 8
CONV_ROWS = 64
VMEM_LIMIT = 60 * 1024 * 1024

BF16 = jnp.bfloat16
F32 = jnp.float32


def _dot(a, b):
    return jnp.dot(a, b, preferred_element_type=F32)


def _dot_nt(a, b):
    return lax.dot_general(a, b, (((1,), (1,)), ((), ())), preferred_element_type=F32)


def _dot_tn(a, b):
    return lax.dot_general(a, b, (((0,), (0,)), ((), ())), preferred_element_type=F32)


def _silu_of_half(h):
    return h * jnp.tanh(h) + h


def _silu(x):
    return _silu_of_half(0.5 * x)


def _rms(x, w):
    return x * lax.rsqrt(jnp.mean(x * x, axis=-1, keepdims=True) + EPS) * w


def _softplus(x):
    return jnp.maximum(x, 0.0) + jnp.log1p(jnp.exp(-jnp.abs(x)))


def _resident(shape):
    nd = len(shape)
    return pl.BlockSpec(shape, lambda *_: (0,) * nd, pipeline_mode=pl.Buffered(1))


def _weight(shape):
    return _resident((shape[0], shape[1] + WPAD))


def _cast_weight_kernel(w_ref, o_ref):
    n = w_ref.shape[1]
    o_ref[:, 0:n] = w_ref[...].astype(BF16)
    if o_ref.shape[1] > n:
        o_ref[:, n:] = jnp.zeros((o_ref.shape[0], o_ref.shape[1] - n), BF16)


def _cast_weight(w, l, pad=WPAD, steps=4):
    _, k, n = w.shape
    rows = k // steps
    return pl.pallas_call(
        _cast_weight_kernel,
        grid=(k // rows,),
        in_specs=[pl.BlockSpec((None, rows, n), lambda i: (l, i, 0))],
        out_specs=pl.BlockSpec((rows, n + pad), lambda i: (i, 0)),
        out_shape=jax.ShapeDtypeStruct((k, n + pad), BF16),
        compiler_params=pltpu.CompilerParams(dimension_semantics=("arbitrary",)),
        name="cast_weight",
    )(w)


def _mod_kernel(c_ref, w_ref, b_ref, o_ref):
    a = _silu(c_ref[...]).astype(BF16)
    o_ref[...] = _dot(a, w_ref[...].astype(BF16)) + b_ref[...]


def _modulation(c, w_ada, b_ada):
    m = c.shape[0]
    n = w_ada.shape[1]
    tn = 2048
    return pl.pallas_call(
        _mod_kernel,
        grid=(n // tn,),
        in_specs=[
            pl.BlockSpec((m, D_MODEL), lambda j: (0, 0)),
            pl.BlockSpec((D_MODEL, tn), lambda j: (0, j)),
            pl.BlockSpec((1, tn), lambda j: (0, j)),
        ],
        out_specs=pl.BlockSpec((m, tn), lambda j: (0, j)),
        out_shape=jax.ShapeDtypeStruct((m, n), F32),
        name="adaln_mod",
    )(c, w_ada, b_ada.reshape(1, n))


WCOL_G = COL_DT
W_COLS = WCOL_G + 2 * D_MODEL
PREP_ROWS = 512
W_ALL_COLS = W_COLS + PREP_ROWS


def _prep_w_in_kernel(wt_ref, wt_dt_ref, w_ref, wdt_ref):
    i = pl.program_id(0)
    last = pl.num_programs(0) - 1
    halve = (i >= COL_Z // PREP_ROWS) & (i < COL_XBC // PREP_ROWS)
    scale = jnp.where(halve, 0.5, 1.0)

    @pl.when(i < last)
    def _():
        w_ref[...] = (scale * wt_ref[0]).astype(BF16).T

    @pl.when(i == last)
    def _():
        w_ref[...] = jnp.zeros_like(w_ref)

    @pl.when(i == 0)
    def _():
        d = wt_dt_ref[0].T[:, 0:HEADS]
        wdt_ref[...] = jnp.concatenate([d] * (LANES // HEADS), axis=1).astype(BF16)


def _prep_w_in(w_in, l):
    wt = jnp.swapaxes(w_in, 1, 2)
    n_main = COL_DT // PREP_ROWS
    n_blocks = W_COLS // PREP_ROWS
    elem_block = (pl.Element(1), pl.Element(PREP_ROWS), pl.Element(D_MODEL))
    dt_block = (pl.Element(1), pl.Element(LANES), pl.Element(D_MODEL))

    def src_row(i):
        j = jnp.minimum(i, n_blocks - 1)
        per = PREP_ROWS // SUBLANES
        tile_row = jnp.where(j < n_main, j * per, COL_GATE // SUBLANES + (j - n_main) * per)
        return tile_row * SUBLANES

    w_all, wdt = pl.pallas_call(
        _prep_w_in_kernel,
        grid=(W_ALL_COLS // PREP_ROWS,),
        in_specs=[
            pl.BlockSpec(elem_block, lambda i: (l, src_row(i), 0)),
            pl.BlockSpec(dt_block, lambda i: (l, COL_DT, 0)),
        ],
        out_specs=[
            pl.BlockSpec((D_MODEL, PREP_ROWS), lambda i: (0, i)),
            pl.BlockSpec((D_MODEL, LANES), lambda i: (0, 0)),
        ],
        out_shape=[
            jax.ShapeDtypeStruct((D_MODEL, W_ALL_COLS), BF16),
            jax.ShapeDtypeStruct((D_MODEL, LANES), BF16),
        ],
        compiler_params=pltpu.CompilerParams(dimension_semantics=("arbitrary",)),
        name="prep_w_in",
    )(wt, wt)
    return dict(w_all=w_all, wdt=wdt)


def _norm_mod(x, sh, sc, w):
    return (_rms(x, w) * (1.0 + sc) + sh).astype(BF16)


def _pool_group_out(pooled, g, wpool_ref, pscale_ref):
    sl = slice(g * POOL_GW, (g + 1) * POOL_GW)
    return _dot(pooled.astype(BF16), wpool_ref[g]) * pscale_ref[:, sl]


def _mix_in_prompt_kernel(x_ref, sh_ref, sc_ref, n1w_ref, w_ref, wdt_ref,
                          dtb_ref, wpool_ref, pscale_ref, cw_ref, cb_ref,
                          ag_ref, gb_ref, sz_ref, xc_ref, dt_ref, pst_ref, cst_ref,
                          ubuf, xbuf, *, tm):
    i = pl.program_id(1)

    @pl.when(i == 0)
    def _():
        ubuf[0:POOL_HALO, :] = jnp.zeros((POOL_HALO, D_MODEL), F32)
        xbuf[0:CONV_HALO, :] = jnp.zeros((CONV_HALO, CONV_DIM), F32)

    @pl.when(i > 0)
    def _():
        ubuf[0:POOL_HALO, :] = ubuf[tm:tm + POOL_HALO, :]
        xbuf[0:CONV_HALO, :] = xbuf[tm:tm + CONV_HALO, :]

    hb = _norm_mod(x_ref[...], sh_ref[...], sc_ref[...], n1w_ref[...])
    dt_ref[...] = _softplus(_dot(hb, wdt_ref[...]) + dtb_ref[...])

    pos = i * tm + lax.broadcasted_iota(jnp.int32, (tm, POOL_GW), 0)

    def pool_group(g):
        w = POOL_WINDOWS[g]
        sl = slice(g * POOL_GW, (g + 1) * POOL_GW)
        ubuf[POOL_HALO:POOL_HALO + tm, sl] = _dot(hb, w_ref[:, sl])
        full = ubuf[:, sl]
        acc = full
        span = 1
        while span < w:
            acc = acc + pltpu.roll(acc, span, axis=0)
            span *= 2
        cur = full[POOL_HALO:, :]
        cnt = jnp.minimum(pos + 1, w).astype(F32)
        a = _pool_group_out(acc[POOL_HALO:, :] / cnt - cur, g, wpool_ref, pscale_ref)
        gate_a = _dot(hb, w_ref[:, WCOL_G + g * POOL_GW:WCOL_G + (g + 1) * POOL_GW])
        ag_ref[:, sl] = jax.nn.sigmoid(gate_a) * a

    cchunk = 1024

    def gate_b_chunk(c0):
        gb_ref[:, c0:c0 + cchunk] = jax.nn.sigmoid(
            _dot(hb, w_ref[:, WCOL_G + D_MODEL + c0:WCOL_G + D_MODEL + c0 + cchunk]))

    def z_chunk(c0):
        sz_ref[:, c0:c0 + cchunk] = _silu_of_half(_dot(hb, w_ref[:, COL_Z + c0:COL_Z + c0 + cchunk]))

    def conv_chunk(c0):
        for n0 in range(c0, c0 + cchunk, MXU_N):
            xbuf[CONV_HALO:CONV_HALO + tm, n0:n0 + MXU_N] = _dot(
                hb, w_ref[:, COL_XBC + n0:COL_XBC + n0 + MXU_N])
        for l0 in range(c0, c0 + cchunk, LANES):
            sl = slice(l0, l0 + LANES)
            taps = 0.5 * cw_ref[:, sl]
            bias = 0.5 * cb_ref[:, sl]
            for r0 in range(0, tm, CONV_ROWS):
                src = xbuf[r0:r0 + CONV_HALO + CONV_ROWS, sl]
                acc = bias + src[CONV_HALO:, :] * taps[CONV_W - 1:CONV_W, :]
                for k in range(CONV_W - 1):
                    back = CONV_W - 1 - k
                    acc = acc + pltpu.roll(src, back, axis=0)[CONV_HALO:, :] * taps[k:k + 1, :]
                xc_ref[r0:r0 + CONV_ROWS, sl] = _silu_of_half(acc)

    light = ([functools.partial(pool_group, g) for g in range(len(POOL_WINDOWS))]
             + [functools.partial(gate_b_chunk, c0) for c0 in range(0, D_MODEL, cchunk)]
             + [functools.partial(z_chunk, c0) for c0 in range(0, D_INNER, cchunk)])
    heavy = [functools.partial(conv_chunk, c0) for c0 in range(0, CONV_DIM, cchunk)]
    for k in range(max(len(light), len(heavy))):
        if k < len(light):
            light[k]()
        if k < len(heavy):
            heavy[k]()

    @pl.when(i == pl.num_programs(1) - 1)
    def _():
        pst_ref[...] = ubuf[POOL_HALO + tm - POOL_BUF:POOL_HALO + tm, :]
        cst_ref[...] = xbuf[CONV_HALO + tm - (CONV_W - 1):CONV_HALO + tm, :]


def _mix_in_prompt(x, mod3, lw, tm=256):
    bsz, L, _ = x.shape
    nt = L // tm
    tok = lambda w: pl.BlockSpec((None, tm, w), lambda b, i: (b, i, 0))
    modspec = lambda k: pl.BlockSpec((None, 1, D_MODEL), lambda b, i, k=k: (b, 0, k))
    outs = pl.pallas_call(
        functools.partial(_mix_in_prompt_kernel, tm=tm),
        grid=(bsz, nt),
        in_specs=[
            tok(D_MODEL), modspec(0), modspec(1),
            _resident((1, D_MODEL)),
            _resident((D_MODEL, W_ALL_COLS)),
            _resident((D_MODEL, LANES)), _resident((1, LANES)),
            _resident((len(POOL_WINDOWS), POOL_GW, POOL_GW)), _resident((1, D_MODEL)),
            _resident((CONV_W, CONV_DIM)), _resident((1, CONV_DIM)),
        ],
        out_specs=[
            tok(D_MODEL), tok(D_MODEL), tok(D_INNER), tok(CONV_DIM), tok(LANES),
            pl.BlockSpec((None, POOL_BUF, D_MODEL), lambda b, i: (b, 0, 0)),
            pl.BlockSpec((None, CONV_W - 1, CONV_DIM), lambda b, i: (b, 0, 0)),
        ],
        out_shape=[
            jax.ShapeDtypeStruct((bsz, L, D_MODEL), F32),
            jax.ShapeDtypeStruct((bsz, L, D_MODEL), F32),
            jax.ShapeDtypeStruct((bsz, L, D_INNER), F32),
            jax.ShapeDtypeStruct((bsz, L, CONV_DIM), F32),
            jax.ShapeDtypeStruct((bsz, L, LANES), F32),
            jax.ShapeDtypeStruct((bsz, POOL_BUF, D_MODEL), F32),
            jax.ShapeDtypeStruct((bsz, CONV_W - 1, CONV_DIM), F32),
        ],
        scratch_shapes=[
            pltpu.VMEM((POOL_HALO + tm, D_MODEL), F32),
            pltpu.VMEM((CONV_HALO + tm, CONV_DIM), F32),
        ],
        compiler_params=pltpu.CompilerParams(
            dimension_semantics=("arbitrary", "arbitrary"), vmem_limit_bytes=VMEM_LIMIT),
        name="mix_in_prompt",
    )(x, mod3, mod3, lw["n1w"], lw["w_all"], lw["wdt"], lw["dtb"],
      lw["wpool"], lw["pscale"], lw["cw"], lw["cb"])
    return outs


def _mixer_tail(y, sz, ag, gb, x, g1, nw, wproj_ref, wout_ref):
    yn = _rms(y * sz, nw).astype(BF16)
    b_out = _dot(yn, wproj_ref[:, 0:D_MODEL])
    mix = (ag + gb * b_out).astype(BF16)
    return x + g1 * _dot(mix, wout_ref[:, 0:D_MODEL])


def _head_pair_expand(v, h0):
    rows = v.shape[0]
    lane = lax.broadcasted_iota(jnp.int32, (rows, LANES), 1)
    lo = jnp.broadcast_to(v[:, h0:h0 + 1], (rows, LANES))
    hi = jnp.broadcast_to(v[:, h0 + 1:h0 + 2], (rows, LANES))
    return jnp.where(lane < HEADDIM, lo, hi)


def _group_expand(v, g):
    return jnp.concatenate(
        [_head_pair_expand(v, HPG * g + 2 * t) for t in range(GROUP_W // LANES)], axis=1)


def _ssd_prompt_kernel(xc_ref, dt_ref, sz_ref, ag_ref, gb_ref, x_ref, g1_ref, alog_ref, dexp_ref,
                       hexp_ref, nw_ref, wproj_ref, wout_ref,
                       x1_ref, hout_ref, ht_scr, y_scr, *, rows):
    step = pl.program_id(1)

    @pl.when(step == 0)
    def _():
        ht_scr[...] = jnp.zeros_like(ht_scr)

    q = CHUNK
    row = lax.broadcasted_iota(jnp.int32, (q, q), 0)
    lane = lax.broadcasted_iota(jnp.int32, (q, q), 1)
    causal = row >= lane
    tri = causal.astype(BF16)
    tri3 = jnp.concatenate([tri, tri, tri], axis=1)
    a_neg = -jnp.exp(alog_ref[...]) * LOG2_E
    lane_g = lax.broadcasted_iota(jnp.int32, (q, GROUP_W), 1)

    for r0 in range(0, rows, q):
        rs = slice(r0, r0 + q)
        dt = dt_ref[rs, :]
        d_a = dt * a_neg
        a_hi = d_a.astype(BF16)
        a_r1 = d_a - a_hi.astype(F32)
        a_mid = a_r1.astype(BF16)
        a_lo = (a_r1 - a_mid.astype(F32)).astype(BF16)
        acs = _dot(tri3, jnp.concatenate([a_hi, a_mid, a_lo], axis=0))
        acs_t = acs.T
        dt_t = dt.T
        w_t = dt_t * jnp.exp2(acs_t[:, q - 1:q] - acs_t)

        e_acs = jnp.exp2(acs)
        e_hi = e_acs.astype(BF16).astype(F32)
        e_r1 = e_acs - e_hi
        e_mid = e_r1.astype(BF16).astype(F32)
        e_lo = e_r1 - e_mid
        pieces = jnp.where(lane < HEADS, e_hi,
                           jnp.where(lane < 2 * HEADS, e_mid,
                                     jnp.where(lane < 3 * HEADS, e_lo, 0.0))).astype(BF16)

        c_bf, bt_f, cbs = [], [], []
        for g in range(GROUPS):
            b0 = D_INNER + g * D_STATE
            c0 = D_INNER + GROUPS * D_STATE + g * D_STATE
            bt = xc_ref[rs, b0:b0 + D_STATE].T
            cg = xc_ref[rs, c0:c0 + D_STATE].astype(BF16)
            c_bf.append(cg)
            bt_f.append(bt)
            cbs.append(_dot(cg, bt.astype(BF16)))

        for g in range(GROUPS):
            gs = slice(g * GROUP_W, (g + 1) * GROUP_W)
            x_g = xc_ref[rs, gs]
            x_bf = x_g.astype(BF16)
            scores, btw, blocks = [], [], []
            for r in range(HPG):
                h = HPG * g + r
                seg = acs[:, h:h + 1] - acs_t[h:h + 1, :]
                decay = jnp.exp2(jnp.where(causal, seg, -jnp.inf))
                scores.append((cbs[g] * decay * dt_t[h:h + 1, :]).astype(BF16))
                btw.append((bt_f[g] * w_t[h:h + 1, :]).astype(BF16))
                in_head = (lane_g >= r * HEADDIM) & (lane_g < (r + 1) * HEADDIM)
                blocks.append(jnp.where(in_head, x_bf, jnp.zeros_like(x_bf)))
            lhs = jnp.concatenate(
                [jnp.concatenate(scores, axis=1), jnp.concatenate(btw, axis=1)], axis=0)
            both = _dot(lhs, jnp.concatenate(blocks, axis=0))
            ht_prev = ht_scr[g * D_STATE:(g + 1) * D_STATE, :]
            e_exp = _dot(pieces, hexp_ref[:, gs])
            y_off = _dot(c_bf[g], ht_prev.astype(BF16)) * e_exp
            y_scr[rs, gs] = both[0:q, :] + y_off + dexp_ref[:, gs] * x_g
            ht_scr[g * D_STATE:(g + 1) * D_STATE, :] = (
                ht_prev * e_exp[q - 1:q, :] + both[q:2 * q, :])

    x1_ref[...] = _mixer_tail(y_scr[...], sz_ref[...], ag_ref[...], gb_ref[...], x_ref[...],
                              g1_ref[...], nw_ref[...], wproj_ref, wout_ref)

    @pl.when(step == pl.num_programs(1) - 1)
    def _():
        for g in range(GROUPS):
            hout_ref[g * GROUP_W:(g + 1) * GROUP_W, :] = ht_scr[g * D_STATE:(g + 1) * D_STATE, :].T


def _ssd_prompt(xc, dt, sz, ag, gb, x, mod3, lw, rows=4 * CHUNK):
    bsz, L, _ = x.shape
    tok = lambda w: pl.BlockSpec((None, rows, w), lambda b, c: (b, c, 0))
    return pl.pallas_call(
        functools.partial(_ssd_prompt_kernel, rows=rows),
        grid=(bsz, L // rows),
        in_specs=[
            tok(CONV_DIM), tok(LANES), tok(D_INNER), tok(D_MODEL), tok(D_MODEL), tok(D_MODEL),
            pl.BlockSpec((None, 1, D_MODEL), lambda b, c: (b, 0, 2)),
            _resident((1, LANES)), _resident((1, D_INNER)), _resident((LANES, D_INNER)),
            _resident((1, D_INNER)),
            _weight((D_INNER, D_MODEL)), _weight((D_MODEL, D_MODEL)),
        ],
        out_specs=[
            tok(D_MODEL),
            pl.BlockSpec((None, HEADS * HEADDIM, D_STATE), lambda b, c: (b, 0, 0)),
        ],
        out_shape=[
            jax.ShapeDtypeStruct((bsz, L, D_MODEL), F32),
            jax.ShapeDtypeStruct((bsz, HEADS * HEADDIM, D_STATE), F32),
        ],
        scratch_shapes=[
            pltpu.VMEM((GROUPS * D_STATE, GROUP_W), F32),
            pltpu.VMEM((rows, D_INNER), F32),
        ],
        compiler_params=pltpu.CompilerParams(
            dimension_semantics=("arbitrary", "arbitrary"), vmem_limit_bytes=VMEM_LIMIT),
        name="ssd_prompt",
    )(xc, dt, sz, ag, gb, x, mod3, lw["alog"], lw["dexp"], lw["hexp"], lw["ssd_nw"], lw["wproj"],
      lw["wout"])


def _ffn_kernel(x_ref, sh_ref, sc_ref, g2_ref, n2w_ref, win_ref, wout_ref, fnw_ref, o_ref, act_scr,
                *, final):
    x = x_ref[...]
    hb = _norm_mod(x, sh_ref[...], sc_ref[...], n2w_ref[...])
    cw = 256
    for c0 in range(0, D_FF, cw):
        gt = _dot(hb, win_ref[:, c0:c0 + cw])
        up = _dot(hb, win_ref[:, D_FF + c0:D_FF + c0 + cw])
        act_scr[:, c0:c0 + cw] = (_silu(gt) * up).astype(BF16)
    x2 = x + g2_ref[...] * _dot(act_scr[...], wout_ref[:, 0:D_MODEL])
    o_ref[...] = _rms(x2, fnw_ref[...]) if final else x2


def _ffn(x2d, mods, lw, fnw, final, tm):
    t = x2d.shape[0]
    mod_arr, modspec = mods
    return pl.pallas_call(
        functools.partial(_ffn_kernel, final=final),
        grid=(t // tm,),
        in_specs=[
            pl.BlockSpec((tm, D_MODEL), lambda i: (i, 0)),
            modspec(3), modspec(4), modspec(5),
            _resident((1, D_MODEL)),
            _resident((D_MODEL, 2 * D_FF)), _weight((D_FF, D_MODEL)),
            _resident((1, D_MODEL)),
        ],
        out_specs=pl.BlockSpec((tm, D_MODEL), lambda i: (i, 0)),
        out_shape=jax.ShapeDtypeStruct((t, D_MODEL), F32),
        scratch_shapes=[pltpu.VMEM((tm, D_FF), BF16)],
        compiler_params=pltpu.CompilerParams(
            dimension_semantics=("arbitrary",), vmem_limit_bytes=VMEM_LIMIT),
        name="ffn",
    )(x2d, mod_arr, mod_arr, mod_arr, lw["n2w"], lw["wffn_in"], lw["wffn_out"], fnw)


def _mix_in_sample_kernel(x_ref, sh_ref, sc_ref, n1w_ref, w_ref, wdt_ref,
                          dtb_ref, wpool_ref, pscale_ref, cw_ref, cb_ref, spool_ref, sconv_ref,
                          ag_ref, gb_ref, sz_ref, xc_ref, dt_ref, pnew_ref, cnew_ref):
    hb = _norm_mod(x_ref[...], sh_ref[...], sc_ref[...], n1w_ref[...])
    u = _dot(hb, w_ref[:, 0:COL_Z])
    pnew_ref[0:POOL_BUF - 1] = spool_ref[1:POOL_BUF]
    pnew_ref[POOL_BUF - 1] = u
    sz_ref[...] = _silu_of_half(_dot(hb, w_ref[:, COL_Z:COL_XBC]))
    xbc = _dot(hb, w_ref[:, COL_XBC:COL_DT])
    cnew_ref[0:CONV_W - 2] = sconv_ref[1:CONV_W - 1]
    cnew_ref[CONV_W - 2] = xbc
    gates = jax.nn.sigmoid(_dot(hb, w_ref[:, WCOL_G:W_COLS]))
    gb_ref[...] = gates[:, D_MODEL:]
    dt_ref[...] = _softplus(_dot(hb, wdt_ref[...]) + dtb_ref[...])

    for g, w in enumerate(POOL_WINDOWS):
        sl = slice(g * POOL_GW, (g + 1) * POOL_GW)
        cur = u[:, sl]
        acc = cur
        for k in range(1, w):
            acc = acc + spool_ref[POOL_BUF - k, :, sl]
        cnt = float(min(PAST_LEN + 1, w))
        a = _pool_group_out(acc / cnt - cur, g, wpool_ref, pscale_ref)
        ag_ref[:, sl] = gates[:, sl] * a

    acc = cb_ref[...] + xbc * cw_ref[CONV_W - 1:CONV_W, :]
    for k in range(CONV_W - 1):
        acc = acc + sconv_ref[k] * cw_ref[k:k + 1, :]
    xc_ref[...] = _silu(acc)


def _mix_in_sample(x2d, mod_s, lw, spool_t, sconv_t):
    n = x2d.shape[0]
    full = lambda r, c: pl.BlockSpec((r, c), lambda i: (0, 0), pipeline_mode=pl.Buffered(1))
    full3 = lambda a, r, c: pl.BlockSpec((a, r, c), lambda i: (0, 0, 0), pipeline_mode=pl.Buffered(1))
    modspec = lambda k: pl.BlockSpec((n, D_MODEL), lambda i, k=k: (0, k),
                                     pipeline_mode=pl.Buffered(1))
    return pl.pallas_call(
        _mix_in_sample_kernel,
        grid=(1,),
        in_specs=[
            full(n, D_MODEL), modspec(0), modspec(1),
            _resident((1, D_MODEL)),
            _resident((D_MODEL, W_ALL_COLS)),
            _resident((D_MODEL, LANES)), _resident((1, LANES)),
            _resident((len(POOL_WINDOWS), POOL_GW, POOL_GW)), _resident((1, D_MODEL)),
            _resident((CONV_W, CONV_DIM)), _resident((1, CONV_DIM)),
            full3(POOL_BUF, n, D_MODEL), full3(CONV_W - 1, n, CONV_DIM),
        ],
        out_specs=[
            full(n, D_MODEL), full(n, D_MODEL), full(n, D_INNER), full(n, CONV_DIM), full(n, LANES),
            full3(POOL_BUF, n, D_MODEL), full3(CONV_W - 1, n, CONV_DIM),
        ],
        out_shape=[
            jax.ShapeDtypeStruct((n, D_MODEL), F32),
            jax.ShapeDtypeStruct((n, D_MODEL), F32),
            jax.ShapeDtypeStruct((n, D_INNER), F32),
            jax.ShapeDtypeStruct((n, CONV_DIM), F32),
            jax.ShapeDtypeStruct((n, LANES), F32),
            jax.ShapeDtypeStruct((POOL_BUF, n, D_MODEL), F32),
            jax.ShapeDtypeStruct((CONV_W - 1, n, CONV_DIM), F32),
        ],
        compiler_params=pltpu.CompilerParams(
            dimension_semantics=("arbitrary",), vmem_limit_bytes=VMEM_LIMIT),
        name="mix_in_sample",
    )(x2d, mod_s, mod_s, lw["n1w"], lw["w_all"], lw["wdt"], lw["dtb"],
      lw["wpool"], lw["pscale"], lw["cw"], lw["cb"], spool_t, sconv_t)


def _ssm_step_kernel(h_ref, xc_ref, dt_ref, alog_ref, dexp_ref, hout_ref, y_ref, *, bb):
    lane = lax.broadcasted_iota(jnp.int32, (bb, LANES), 1)
    head_ok = lane < HEADS
    dt = jnp.where(head_ok, dt_ref[...], 0.0)
    a_neg = jnp.where(head_ok[0:1, :], -jnp.exp(alog_ref[...]), 0.0)
    dec = jnp.exp(dt * a_neg)
    xs = xc_ref[:, 0:D_INNER]
    dt_exp = jnp.concatenate([_group_expand(dt, g) for g in range(GROUPS)], axis=1)
    pad = jnp.zeros((bb, D_INNER), F32)
    xdt = jnp.concatenate([xs * dt_exp, pad], axis=0).astype(BF16)
    bm = xc_ref[:, D_INNER:D_INNER + GROUPS * D_STATE]
    cm = xc_ref[:, D_INNER + GROUPS * D_STATE:]
    rowid = lax.broadcasted_iota(jnp.int32, (2 * bb, D_STATE), 0)
    rowid_y = lax.broadcasted_iota(jnp.int32, (2 * bb, GROUP_W), 0)
    padn = jnp.zeros((bb, D_STATE), F32)

    y_acc = [jnp.zeros((2 * bb, GROUP_W), F32) for _ in range(GROUPS)]
    for j in range(bb):
        for g in range(GROUPS):
            rs = slice(g * GROUP_W, (g + 1) * GROUP_W)
            ns = slice(g * D_STATE, (g + 1) * D_STATE)
            h_g = h_ref[j, rs, :]
            dcol = jnp.concatenate(
                [jnp.broadcast_to(dec[j:j + 1, HPG * g + r:HPG * g + r + 1], (HEADDIM, D_STATE))
                 for r in range(HPG)], axis=0)
            b16 = jnp.concatenate([bm[:, ns], padn], axis=0)
            b_j = jnp.where(rowid == j, b16, 0.0).astype(BF16)
            new = h_g * dcol + _dot_tn(xdt[:, rs], b_j)
            hout_ref[j, rs, :] = new
            c16 = jnp.concatenate([cm[:, ns], padn], axis=0).astype(BF16)
            y_all = _dot_nt(c16, new.astype(BF16))
            y_acc[g] = y_acc[g] + jnp.where(rowid_y == j, y_all, 0.0)
    y = jnp.concatenate(y_acc, axis=1)[0:bb, :]
    y_ref[...] = y + dexp_ref[...] * xs


def _ssm_step(h3, xc, dt, lw, bb=8):
    n = h3.shape[0]
    return pl.pallas_call(
        functools.partial(_ssm_step_kernel, bb=bb),
        grid=(n // bb,),
        in_specs=[
            pl.BlockSpec((bb, HEADS * HEADDIM, D_STATE), lambda i: (i, 0, 0)),
            pl.BlockSpec((bb, CONV_DIM), lambda i: (i, 0)),
            pl.BlockSpec((bb, LANES), lambda i: (i, 0)),
            _resident((1, LANES)), _resident((1, D_INNER)),
        ],
        out_specs=[
            pl.BlockSpec((bb, HEADS * HEADDIM, D_STATE), lambda i: (i, 0, 0)),
            pl.BlockSpec((bb, D_INNER), lambda i: (i, 0)),
        ],
        out_shape=[
            jax.ShapeDtypeStruct(h3.shape, F32),
            jax.ShapeDtypeStruct((n, D_INNER), F32),
        ],
        compiler_params=pltpu.CompilerParams(
            dimension_semantics=("arbitrary",), vmem_limit_bytes=VMEM_LIMIT),
        name="ssm_step",
    )(h3, xc, dt, lw["alog"], lw["dexp"])


def _mix_out_sample_kernel(y_ref, sz_ref, ag_ref, gb_ref, x_ref, g1_ref, nw_ref, wproj_ref, wout_ref,
                           o_ref):
    o_ref[...] = _mixer_tail(y_ref[...], sz_ref[...], ag_ref[...], gb_ref[...], x_ref[...],
                             g1_ref[...], nw_ref[...], wproj_ref, wout_ref)


def _mix_out_sample(y, sz, ag, gb, x2d, mod_s, lw):
    n = x2d.shape[0]
    full = lambda c: pl.BlockSpec((n, c), lambda i: (0, 0))
    return pl.pallas_call(
        _mix_out_sample_kernel,
        grid=(1,),
        in_specs=[
            full(D_INNER), full(D_INNER), full(D_MODEL), full(D_MODEL), full(D_MODEL),
            pl.BlockSpec((n, D_MODEL), lambda i: (0, 2)),
            _resident((1, D_INNER)), _weight((D_INNER, D_MODEL)), _weight((D_MODEL, D_MODEL)),
        ],
        out_specs=full(D_MODEL),
        out_shape=jax.ShapeDtypeStruct((n, D_MODEL), F32),
        compiler_params=pltpu.CompilerParams(
            dimension_semantics=("arbitrary",), vmem_limit_bytes=VMEM_LIMIT),
        name="mix_out_sample",
    )(y, sz, ag, gb, x2d, mod_s, lw["ssd_nw"], lw["wproj"], lw["wout"])


def _layer_weights(l, w_in, w_pool, pool_scale, conv_w, conv_b, dt_bias, A_log, D_skip, ssd_norm_w,
                   w_ssd_proj, w_out, norm1_w, norm2_w, w_ffn_in, w_ffn_out):
    rep = LANES // HEADS
    lane_id = jnp.arange(LANES)[:, None]
    col_head = jnp.arange(D_INNER)[None, :] // HEADDIM
    hexp = ((lane_id % HEADS == col_head) & (lane_id < 3 * HEADS)).astype(BF16)
    return dict(
        n1w=norm1_w[l].reshape(1, D_MODEL),
        **_prep_w_in(w_in, l),
        dtb=jnp.tile(dt_bias[l], rep).reshape(1, LANES),
        wpool=w_pool[l].astype(BF16),
        pscale=pool_scale[l].reshape(1, D_MODEL),
        cw=conv_w[l],
        cb=conv_b[l].reshape(1, CONV_DIM),
        alog=jnp.tile(A_log[l], rep).reshape(1, LANES),
        hexp=hexp,
        dexp=jnp.repeat(D_skip[l], HEADDIM).reshape(1, D_INNER),
        ssd_nw=ssd_norm_w[l].reshape(1, D_INNER),
        wproj=_cast_weight(w_ssd_proj, l),
        wout=_cast_weight(w_out, l),
        n2w=norm2_w[l].reshape(1, D_MODEL),
        wffn_in=_cast_weight(w_ffn_in, l, pad=0),
        wffn_out=_cast_weight(w_ffn_out, l),
    )


def kernel(x_prompt, x_sample, c_prompt, c_sample, state_pool, state_conv, state_ssm, w_ada, b_ada, norm1_w, w_in, w_pool, pool_scale, conv_w, conv_b, dt_bias, A_log, D_skip, ssd_norm_w, w_ssd_proj, w_out, norm2_w, w_ffn_in, w_ffn_out, final_norm_w):
    depth = w_in.shape[0]
    bsz, L, _ = x_prompt.shape
    nsmp = x_sample.shape[0]
    xp = x_prompt
    xs = x_sample.reshape(nsmp, D_MODEL)
    c_all = jnp.concatenate([c_prompt, c_sample], axis=0)
    fnw = final_norm_w.reshape(1, D_MODEL)
    pool_p, conv_p, ssm_p, pool_s, conv_s, ssm_s = [], [], [], [], [], []
    for l in range(depth):
        final = l == depth - 1
        lw = _layer_weights(l, w_in, w_pool, pool_scale, conv_w, conv_b, dt_bias, A_log, D_skip,
                            ssd_norm_w, w_ssd_proj, w_out, norm1_w, norm2_w, w_ffn_in, w_ffn_out)
        mod = _modulation(c_all, w_ada[l], b_ada[l])
        mod_p = mod[:bsz].reshape(bsz, 1, N_MOD * D_MODEL)
        mod_s = mod[bsz:]

        ag, gb, sz, xc, dt, pst, cst = _mix_in_prompt(xp, mod_p, lw)
        x1, hst = _ssd_prompt(xc, dt, sz, ag, gb, xp, mod_p, lw)
        tm = 1024
        per = L // tm
        p_mods = (mod_p, lambda k: pl.BlockSpec((None, 1, D_MODEL), lambda i, k=k: (i // per, 0, k)))
        xp = _ffn(x1.reshape(bsz * L, D_MODEL), p_mods, lw, fnw, final, tm).reshape(bsz, L, D_MODEL)
        pool_p.append(pst)
        conv_p.append(cst)
        ssm_p.append(hst.reshape(bsz, HEADS, HEADDIM, D_STATE))

        ag, gb, sz, xc, dt, pnew_t, cnew_t = _mix_in_sample(
            xs, mod_s, lw, jnp.swapaxes(state_pool[l], 0, 1), jnp.swapaxes(state_conv[l], 0, 1))
        hnew, y = _ssm_step(state_ssm[l].reshape(nsmp, HEADS * HEADDIM, D_STATE), xc, dt, lw)
        x1s = _mix_out_sample(y, sz, ag, gb, xs, mod_s, lw)
        s_mods = (mod_s, lambda k: pl.BlockSpec((nsmp, D_MODEL), lambda i, k=k: (0, k)))
        xs = _ffn(x1s, s_mods, lw, fnw, final, nsmp)
        pool_s.append(jnp.swapaxes(pnew_t, 0, 1))
        conv_s.append(jnp.swapaxes(cnew_t, 0, 1))
        ssm_s.append(hnew.reshape(nsmp, HEADS, HEADDIM, D_STATE))

    return (xp, xs.reshape(nsmp, 1, D_MODEL), jnp.stack(pool_p), jnp.stack(conv_p), jnp.stack(ssm_p),
            jnp.stack(pool_s), jnp.stack(conv_s), jnp.stack(ssm_s))
```

```python
import functools

import jax
import jax.numpy as jnp
from jax import lax
from jax.experimental import pallas as pl
from jax.experimental.pallas import tpu as pltpu

D_MODEL = 1024
POOL_WINDOWS = (2, 4, 8, 16)
POOL_GW = D_MODEL // len(POOL_WINDOWS)
POOL_BUF = max(POOL_WINDOWS) - 1
D_INNER = 2 * D_MODEL
HEADDIM = 64
HEADS = D_INNER // HEADDIM
GROUPS = 8
HPG = HEADS // GROUPS
D_STATE = 128
CONV_W = 4
CONV_DIM = D_INNER + 2 * GROUPS * D_STATE
CHUNK = 128
D_FF = ((8 * D_MODEL // 3 + 255) // 256) * 256
N_MOD = 6
COL_Z = D_MODEL
COL_XBC = COL_Z + D_INNER
COL_DT = COL_XBC + CONV_DIM
COL_GATE = COL_DT + HEADS
IN_COLS = COL_GATE + 2 * D_MODEL
PAST_LEN = 16384
EPS = 1e-6
LOG2_E = 1.4426950408889634

LANES = 128
SUBLANES = 8
MXU_N = 256
WPAD = LANES
GROUP_W = HPG * HEADDIM
POOL_HALO = 16
CONV_HALO = 8
CONV_ROWS = 64
VMEM_LIMIT = 60 * 1024 * 1024

BF16 = jnp.bfloat16
F32 = jnp.float32


def _dot(a, b):
    return jnp.dot(a, b, preferred_element_type=F32)


def _dot_nt(a, b):
    return lax.dot_general(a, b, (((1,), (1,)), ((), ())), preferred_element_type=F32)


def _dot_tn(a, b):
    return lax.dot_general(a, b, (((0,), (0,)), ((), ())), preferred_element_type=F32)


def _silu_of_half(h):
    return h * jnp.tanh(h) + h


def _silu(x):
    return _silu_of_half(0.5 * x)


def _rms(x, w):
    return x * lax.rsqrt(jnp.mean(x * x, axis=-1, keepdims=True) + EPS) * w


def _softplus(x):
    return jnp.maximum(x, 0.0) + jnp.log1p(jnp.exp(-jnp.abs(x)))


def _resident(shape):
    nd = len(shape)
    return pl.BlockSpec(shape, lambda *_: (0,) * nd, pipeline_mode=pl.Buffered(1))


def _weight(shape):
    return _resident((shape[0], shape[1] + WPAD))


def _cast_weight_kernel(w_ref, o_ref):
    n = w_ref.shape[1]
    o_ref[:, 0:n] = w_ref[...].astype(BF16)
    if o_ref.shape[1] > n:
        o_ref[:, n:] = jnp.zeros((o_ref.shape[0], o_ref.shape[1] - n), BF16)


def _cast_weight(w, l, pad=WPAD, steps=4):
    _, k, n = w.shape
    rows = k // steps
    return pl.pallas_call(
        _cast_weight_kernel,
        grid=(k // rows,),
        in_specs=[pl.BlockSpec((None, rows, n), lambda i: (l, i, 0))],
        out_specs=pl.BlockSpec((rows, n + pad), lambda i: (i, 0)),
        out_shape=jax.ShapeDtypeStruct((k, n + pad), BF16),
        compiler_params=pltpu.CompilerParams(dimension_semantics=("arbitrary",)),
        name="cast_weight",
    )(w)


def _mod_kernel(c_ref, w_ref, b_ref, o_ref):
    a = _silu(c_ref[...]).astype(BF16)
    o_ref[...] = _dot(a, w_ref[...].astype(BF16)) + b_ref[...]


def _modulation(c, w_ada, b_ada):
    m = c.shape[0]
    n = w_ada.shape[1]
    tn = 2048
    return pl.pallas_call(
        _mod_kernel,
        grid=(n // tn,),
        in_specs=[
            pl.BlockSpec((m, D_MODEL), lambda j: (0, 0)),
            pl.BlockSpec((D_MODEL, tn), lambda j: (0, j)),
            pl.BlockSpec((1, tn), lambda j: (0, j)),
        ],
        out_specs=pl.BlockSpec((m, tn), lambda j: (0, j)),
        out_shape=jax.ShapeDtypeStruct((m, n), F32),
        name="adaln_mod",
    )(c, w_ada, b_ada.reshape(1, n))


WCOL_G = COL_DT
W_COLS = WCOL_G + 2 * D_MODEL
PREP_ROWS = 512
W_ALL_COLS = W_COLS + PREP_ROWS


def _prep_w_in_kernel(wt_ref, wt_dt_ref, w_ref, wdt_ref):
    i = pl.program_id(0)
    last = pl.num_programs(0) - 1
    halve = (i >= COL_Z // PREP_ROWS) & (i < COL_XBC // PREP_ROWS)
    scale = jnp.where(halve, 0.5, 1.0)

    @pl.when(i < last)
    def _():
        w_ref[...] = (scale * wt_ref[0]).astype(BF16).T

    @pl.when(i == last)
    def _():
        w_ref[...] = jnp.zeros_like(w_ref)

    @pl.when(i == 0)
    def _():
        d = wt_dt_ref[0].T[:, 0:HEADS]
        wdt_ref[...] = jnp.concatenate([d] * (LANES // HEADS), axis=1).astype(BF16)


def _prep_w_in(w_in, l):
    wt = jnp.swapaxes(w_in, 1, 2)
    n_main = COL_DT // PREP_ROWS
    n_blocks = W_COLS // PREP_ROWS
    elem_block = (pl.Element(1), pl.Element(PREP_ROWS), pl.Element(D_MODEL))
    dt_block = (pl.Element(1), pl.Element(LANES), pl.Element(D_MODEL))

    def src_row(i):
        j = jnp.minimum(i, n_blocks - 1)
        per = PREP_ROWS // SUBLANES
        tile_row = jnp.where(j < n_main, j * per, COL_GATE // SUBLANES + (j - n_main) * per)
        return tile_row * SUBLANES

    w_all, wdt = pl.pallas_call(
        _prep_w_in_kernel,
        grid=(W_ALL_COLS // PREP_ROWS,),
        in_specs=[
            pl.BlockSpec(elem_block, lambda i: (l, src_row(i), 0)),
            pl.BlockSpec(dt_block, lambda i: (l, COL_DT, 0)),
        ],
        out_specs=[
            pl.BlockSpec((D_MODEL, PREP_ROWS), lambda i: (0, i)),
            pl.BlockSpec((D_MODEL, LANES), lambda i: (0, 0)),
        ],
        out_shape=[
            jax.ShapeDtypeStruct((D_MODEL, W_ALL_COLS), BF16),
            jax.ShapeDtypeStruct((D_MODEL, LANES), BF16),
        ],
        compiler_params=pltpu.CompilerParams(dimension_semantics=("arbitrary",)),
        name="prep_w_in",
    )(wt, wt)
    return dict(w_all=w_all, wdt=wdt)


def _norm_mod(x, sh, sc, w):
    return (_rms(x, w) * (1.0 + sc) + sh).astype(BF16)


def _pool_group_out(pooled, g, wpool_ref, pscale_ref):
    sl = slice(g * POOL_GW, (g + 1) * POOL_GW)
    return _dot(pooled.astype(BF16), wpool_ref[g]) * pscale_ref[:, sl]


def _mix_in_prompt_kernel(x_ref, sh_ref, sc_ref, n1w_ref, w_ref, wdt_ref,
                          dtb_ref, wpool_ref, pscale_ref, cw_ref, cb_ref,
                          ag_ref, gb_ref, sz_ref, xc_ref, dt_ref, pst_ref, cst_ref,
                          ubuf, xbuf, *, tm):
    i = pl.program_id(1)

    @pl.when(i == 0)
    def _():
        ubuf[0:POOL_HALO, :] = jnp.zeros((POOL_HALO, D_MODEL), F32)
        xbuf[0:CONV_HALO, :] = jnp.zeros((CONV_HALO, CONV_DIM), F32)

    @pl.when(i > 0)
    def _():
        ubuf[0:POOL_HALO, :] = ubuf[tm:tm + POOL_HALO, :]
        xbuf[0:CONV_HALO, :] = xbuf[tm:tm + CONV_HALO, :]

    hb = _norm_mod(x_ref[...], sh_ref[...], sc_ref[...], n1w_ref[...])
    dt_ref[...] = _softplus(_dot(hb, wdt_ref[...]) + dtb_ref[...])

    pos = i * tm + lax.broadcasted_iota(jnp.int32, (tm, POOL_GW), 0)

    def pool_group(g):
        w = POOL_WINDOWS[g]
        sl = slice(g * POOL_GW, (g + 1) * POOL_GW)
        ubuf[POOL_HALO:POOL_HALO + tm, sl] = _dot(hb, w_ref[:, sl])
        full = ubuf[:, sl]
        acc = full
        span = 1
        while span < w:
            acc = acc + pltpu.roll(acc, span, axis=0)
            span *= 2
        cur = full[POOL_HALO:, :]
        cnt = jnp.minimum(pos + 1, w).astype(F32)
        a = _pool_group_out(acc[POOL_HALO:, :] / cnt - cur, g, wpool_ref, pscale_ref)
        gate_a = _dot(hb, w_ref[:, WCOL_G + g * POOL_GW:WCOL_G + (g + 1) * POOL_GW])
        ag_ref[:, sl] = jax.nn.sigmoid(gate_a) * a

    cchunk = 1024

    def gate_b_chunk(c0):
        gb_ref[:, c0:c0 + cchunk] = jax.nn.sigmoid(
            _dot(hb, w_ref[:, WCOL_G + D_MODEL + c0:WCOL_G + D_MODEL + c0 + cchunk]))

    def z_chunk(c0):
        sz_ref[:, c0:c0 + cchunk] = _silu_of_half(_dot(hb, w_ref[:, COL_Z + c0:COL_Z + c0 + cchunk]))

    def conv_chunk(c0):
        for n0 in range(c0, c0 + cchunk, MXU_N):
            xbuf[CONV_HALO:CONV_HALO + tm, n0:n0 + MXU_N] = _dot(
                hb, w_ref[:, COL_XBC + n0:COL_XBC + n0 + MXU_N])
        for l0 in range(c0, c0 + cchunk, LANES):
            sl = slice(l0, l0 + LANES)
            taps = 0.5 * cw_ref[:, sl]
            bias = 0.5 * cb_ref[:, sl]
            for r0 in range(0, tm, CONV_ROWS):
                src = xbuf[r0:r0 + CONV_HALO + CONV_ROWS, sl]
                acc = bias + src[CONV_HALO:, :] * taps[CONV_W - 1:CONV_W, :]
                for k in range(CONV_W - 1):
                    back = CONV_W - 1 - k
                    acc = acc + pltpu.roll(src, back, axis=0)[CONV_HALO:, :] * taps[k:k + 1, :]
                xc_ref[r0:r0 + CONV_ROWS, sl] = _silu_of_half(acc)

    light = ([functools.partial(pool_group, g) for g in range(len(POOL_WINDOWS))]
             + [functools.partial(gate_b_chunk, c0) for c0 in range(0, D_MODEL, cchunk)]
             + [functools.partial(z_chunk, c0) for c0 in range(0, D_INNER, cchunk)])
    heavy = [functools.partial(conv_chunk, c0) for c0 in range(0, CONV_DIM, cchunk)]
    for k in range(max(len(light), len(heavy))):
        if k < len(light):
            light[k]()
        if k < len(heavy):
            heavy[k]()

    @pl.when(i == pl.num_programs(1) - 1)
    def _():
        pst_ref[...] = ubuf[POOL_HALO + tm - POOL_BUF:POOL_HALO + tm, :]
        cst_ref[...] = xbuf[CONV_HALO + tm - (CONV_W - 1):CONV_HALO + tm, :]


def _mix_in_prompt(x, mod3, lw, tm=256):
    bsz, L, _ = x.shape
    nt = L // tm
    tok = lambda w: pl.BlockSpec((None, tm, w), lambda b, i: (b, i, 0))
    modspec = lambda k: pl.BlockSpec((None, 1, D_MODEL), lambda b, i, k=k: (b, 0, k))
    outs = pl.pallas_call(
        functools.partial(_mix_in_prompt_kernel, tm=tm),
        grid=(bsz, nt),
        in_specs=[
            tok(D_MODEL), modspec(0), modspec(1),
            _resident((1, D_MODEL)),
            _resident((D_MODEL, W_ALL_COLS)),
            _resident((D_MODEL, LANES)), _resident((1, LANES)),
            _resident((len(POOL_WINDOWS), POOL_GW, POOL_GW)), _resident((1, D_MODEL)),
            _resident((CONV_W, CONV_DIM)), _resident((1, CONV_DIM)),
        ],
        out_specs=[
            tok(D_MODEL), tok(D_MODEL), tok(D_INNER), tok(CONV_DIM), tok(LANES),
            pl.BlockSpec((None, POOL_BUF, D_MODEL), lambda b, i: (b, 0, 0)),
            pl.BlockSpec((None, CONV_W - 1, CONV_DIM), lambda b, i: (b, 0, 0)),
        ],
        out_shape=[
            jax.ShapeDtypeStruct((bsz, L, D_MODEL), F32),
            jax.ShapeDtypeStruct((bsz, L, D_MODEL), F32),
            jax.ShapeDtypeStruct((bsz, L, D_INNER), F32),
            jax.ShapeDtypeStruct((bsz, L, CONV_DIM), F32),
            jax.ShapeDtypeStruct((bsz, L, LANES), F32),
            jax.ShapeDtypeStruct((bsz, POOL_BUF, D_MODEL), F32),
            jax.ShapeDtypeStruct((bsz, CONV_W - 1, CONV_DIM), F32),
        ],
        scratch_shapes=[
            pltpu.VMEM((POOL_HALO + tm, D_MODEL), F32),
            pltpu.VMEM((CONV_HALO + tm, CONV_DIM), F32),
        ],
        compiler_params=pltpu.CompilerParams(
            dimension_semantics=("arbitrary", "arbitrary"), vmem_limit_bytes=VMEM_LIMIT),
        name="mix_in_prompt",
    )(x, mod3, mod3, lw["n1w"], lw["w_all"], lw["wdt"], lw["dtb"],
      lw["wpool"], lw["pscale"], lw["cw"], lw["cb"])
    return outs


def _mixer_tail(y, sz, ag, gb, x, g1, nw, wproj_ref, wout_ref):
    yn = _rms(y * sz, nw).astype(BF16)
    b_out = _dot(yn, wproj_ref[:, 0:D_MODEL])
    mix = (ag + gb * b_out).astype(BF16)
    return x + g1 * _dot(mix, wout_ref[:, 0:D_MODEL])


def _head_pair_expand(v, h0):
    rows = v.shape[0]
    lane = lax.broadcasted_iota(jnp.int32, (rows, LANES), 1)
    lo = jnp.broadcast_to(v[:, h0:h0 + 1], (rows, LANES))
    hi = jnp.broadcast_to(v[:, h0 + 1:h0 + 2], (rows, LANES))
    return jnp.where(lane < HEADDIM, lo, hi)


def _group_expand(v, g):
    return jnp.concatenate(
        [_head_pair_expand(v, HPG * g + 2 * t) for t in range(GROUP_W // LANES)], axis=1)


def _ssd_prompt_kernel(xc_ref, dt_ref, sz_ref, ag_ref, gb_ref, x_ref, g1_ref, alog_ref, dexp_ref,
                       hexp_ref, nw_ref, wproj_ref, wout_ref,
                       x1_ref, hout_ref, ht_scr, y_scr, *, rows):
    step = pl.program_id(1)

    @pl.when(step == 0)
    def _():
        ht_scr[...] = jnp.zeros_like(ht_scr)

    q = CHUNK
    row = lax.broadcasted_iota(jnp.int32, (q, q), 0)
    lane = lax.broadcasted_iota(jnp.int32, (q, q), 1)
    causal = row >= lane
    tri = causal.astype(BF16)
    tri3 = jnp.concatenate([tri, tri, tri], axis=1)
    a_neg = -jnp.exp(alog_ref[...]) * LOG2_E
    lane_g = lax.broadcasted_iota(jnp.int32, (q, GROUP_W), 1)

    for r0 in range(0, rows, q):
        rs = slice(r0, r0 + q)
        dt = dt_ref[rs, :]
        d_a = dt * a_neg
        a_hi = d_a.astype(BF16)
        a_r1 = d_a - a_hi.astype(F32)
        a_mid = a_r1.astype(BF16)
        a_lo = (a_r1 - a_mid.astype(F32)).astype(BF16)
        acs = _dot(tri3, jnp.concatenate([a_hi, a_mid, a_lo], axis=0))
        acs_t = acs.T
        dt_t = dt.T
        w_t = dt_t * jnp.exp2(acs_t[:, q - 1:q] - acs_t)

        e_acs = jnp.exp2(acs)
        e_hi = e_acs.astype(BF16).astype(F32)
        e_r1 = e_acs - e_hi
        e_mid = e_r1.astype(BF16).astype(F32)
        e_lo = e_r1 - e_mid
        pieces = jnp.where(lane < HEADS, e_hi,
                           jnp.where(lane < 2 * HEADS, e_mid,
                                     jnp.where(lane < 3 * HEADS, e_lo, 0.0))).astype(BF16)

        c_bf, bt_f, cbs = [], [], []
        for g in range(GROUPS):
            b0 = D_INNER + g * D_STATE
            c0 = D_INNER + GROUPS * D_STATE + g * D_STATE
            bt = xc_ref[rs, b0:b0 + D_STATE].T
            cg = xc_ref[rs, c0:c0 + D_STATE].astype(BF16)
            c_bf.append(cg)
            bt_f.append(bt)
            cbs.append(_dot(cg, bt.astype(BF16)))

        for g in range(GROUPS):
            gs = slice(g * GROUP_W, (g + 1) * GROUP_W)
            x_g = xc_ref[rs, gs]
            x_bf = x_g.astype(BF16)
            scores, btw, blocks = [], [], []
            for r in range(HPG):
                h = HPG * g + r
                seg = acs[:, h:h + 1] - acs_t[h:h + 1, :]
                decay = jnp.exp2(jnp.where(causal, seg, -jnp.inf))
                scores.append((cbs[g] * decay * dt_t[h:h + 1, :]).astype(BF16))
                btw.append((bt_f[g] * w_t[h:h + 1, :]).astype(BF16))
                in_head = (lane_g >= r * HEADDIM) & (lane_g < (r + 1) * HEADDIM)
                blocks.append(jnp.where(in_head, x_bf, jnp.zeros_like(x_bf)))
            lhs = jnp.concatenate(
                [jnp.concatenate(scores, axis=1), jnp.concatenate(btw, axis=1)], axis=0)
            both = _dot(lhs, jnp.concatenate(blocks, axis=0))
            ht_prev = ht_scr[g * D_STATE:(g + 1) * D_STATE, :]
            e_exp = _dot(pieces, hexp_ref[:, gs])
            y_off = _dot(c_bf[g], ht_prev.astype(BF16)) * e_exp
            y_scr[rs, gs] = both[0:q, :] + y_off + dexp_ref[:, gs] * x_g
            ht_scr[g * D_STATE:(g + 1) * D_STATE, :] = (
                ht_prev * e_exp[q - 1:q, :] + both[q:2 * q, :])

    x1_ref[...] = _mixer_tail(y_scr[...], sz_ref[...], ag_ref[...], gb_ref[...], x_ref[...],
                              g1_ref[...], nw_ref[...], wproj_ref, wout_ref)

    @pl.when(step == pl.num_programs(1) - 1)
    def _():
        for g in range(GROUPS):
            hout_ref[g * GROUP_W:(g + 1) * GROUP_W, :] = ht_scr[g * D_STATE:(g + 1) * D_STATE, :].T


def _ssd_prompt(xc, dt, sz, ag, gb, x, mod3, lw, rows=4 * CHUNK):
    bsz, L, _ = x.shape
    tok = lambda w: pl.BlockSpec((None, rows, w), lambda b, c: (b, c, 0))
    return pl.pallas_call(
        functools.partial(_ssd_prompt_kernel, rows=rows),
        grid=(bsz, L // rows),
        in_specs=[
            tok(CONV_DIM), tok(LANES), tok(D_INNER), tok(D_MODEL), tok(D_MODEL), tok(D_MODEL),
            pl.BlockSpec((None, 1, D_MODEL), lambda b, c: (b, 0, 2)),
            _resident((1, LANES)), _resident((1, D_INNER)), _resident((LANES, D_INNER)),
            _resident((1, D_INNER)),
            _weight((D_INNER, D_MODEL)), _weight((D_MODEL, D_MODEL)),
        ],
        out_specs=[
            tok(D_MODEL),
            pl.BlockSpec((None, HEADS * HEADDIM, D_STATE), lambda b, c: (b, 0, 0)),
        ],
        out_shape=[
            jax.ShapeDtypeStruct((bsz, L, D_MODEL), F32),
            jax.ShapeDtypeStruct((bsz, HEADS * HEADDIM, D_STATE), F32),
        ],
        scratch_shapes=[
            pltpu.VMEM((GROUPS * D_STATE, GROUP_W), F32),
            pltpu.VMEM((rows, D_INNER), F32),
        ],
        compiler_params=pltpu.CompilerParams(
            dimension_semantics=("arbitrary", "arbitrary"), vmem_limit_bytes=VMEM_LIMIT),
        name="ssd_prompt",
    )(xc, dt, sz, ag, gb, x, mod3, lw["alog"], lw["dexp"], lw["hexp"], lw["ssd_nw"], lw["wproj"],
      lw["wout"])


def _ffn_rows(x, sh, sc, g2, n2w, win_ref, wout_ref, fnw, act_scr, final):
    hb = _norm_mod(x, sh, sc, n2w)
    cw = 256
    for c0 in range(0, D_FF, cw):
        gt = _dot(hb, win_ref[:, c0:c0 + cw])
        up = _dot(hb, win_ref[:, D_FF + c0:D_FF + c0 + cw])
        act_scr[:, c0:c0 + cw] = (_silu(gt) * up).astype(BF16)
    x2 = x + g2 * _dot(act_scr[...], wout_ref[:, 0:D_MODEL])
    return _rms(x2, fnw) if final else x2


def _ffn_kernel(x_ref, sh_ref, sc_ref, g2_ref, n2w_ref, win_ref, wout_ref, fnw_ref, o_ref, act_scr,
                *, final):
    o_ref[...] = _ffn_rows(x_ref[...], sh_ref[...], sc_ref[...], g2_ref[...], n2w_ref[...],
                           win_ref, wout_ref, fnw_ref[...], act_scr, final)


def _ffn(x2d, mods, lw, fnw, final, tm):
    t = x2d.shape[0]
    mod_arr, modspec = mods
    return pl.pallas_call(
        functools.partial(_ffn_kernel, final=final),
        grid=(t // tm,),
        in_specs=[
            pl.BlockSpec((tm, D_MODEL), lambda i: (i, 0)),
            modspec(3), modspec(4), modspec(5),
            _resident((1, D_MODEL)),
            _resident((D_MODEL, 2 * D_FF)), _weight((D_FF, D_MODEL)),
            _resident((1, D_MODEL)),
        ],
        out_specs=pl.BlockSpec((tm, D_MODEL), lambda i: (i, 0)),
        out_shape=jax.ShapeDtypeStruct((t, D_MODEL), F32),
        scratch_shapes=[pltpu.VMEM((tm, D_FF), BF16)],
        compiler_params=pltpu.CompilerParams(
            dimension_semantics=("arbitrary",), vmem_limit_bytes=VMEM_LIMIT),
        name="ffn",
    )(x2d, mod_arr, mod_arr, mod_arr, lw["n2w"], lw["wffn_in"], lw["wffn_out"], fnw)


def _mix_in_sample_kernel(x_ref, sh_ref, sc_ref, n1w_ref, w_ref, wdt_ref,
                          dtb_ref, wpool_ref, pscale_ref, cw_ref, cb_ref, spool_ref, sconv_ref,
                          ag_ref, gb_ref, sz_ref, xc_ref, dt_ref, pnew_ref, cnew_ref):
    hb = _norm_mod(x_ref[...], sh_ref[...], sc_ref[...], n1w_ref[...])
    u = _dot(hb, w_ref[:, 0:COL_Z])
    pnew_ref[0:POOL_BUF - 1] = spool_ref[1:POOL_BUF]
    pnew_ref[POOL_BUF - 1] = u
    sz_ref[...] = _silu_of_half(_dot(hb, w_ref[:, COL_Z:COL_XBC]))
    xbc = _dot(hb, w_ref[:, COL_XBC:COL_DT])
    cnew_ref[0:CONV_W - 2] = sconv_ref[1:CONV_W - 1]
    cnew_ref[CONV_W - 2] = xbc
    gates = jax.nn.sigmoid(_dot(hb, w_ref[:, WCOL_G:W_COLS]))
    gb_ref[...] = gates[:, D_MODEL:]
    dt_ref[...] = _softplus(_dot(hb, wdt_ref[...]) + dtb_ref[...])

    for g, w in enumerate(POOL_WINDOWS):
        sl = slice(g * POOL_GW, (g + 1) * POOL_GW)
        cur = u[:, sl]
        acc = cur
        for k in range(1, w):
            acc = acc + spool_ref[POOL_BUF - k, :, sl]
        cnt = float(min(PAST_LEN + 1, w))
        a = _pool_group_out(acc / cnt - cur, g, wpool_ref, pscale_ref)
        ag_ref[:, sl] = gates[:, sl] * a

    acc = cb_ref[...] + xbc * cw_ref[CONV_W - 1:CONV_W, :]
    for k in range(CONV_W - 1):
        acc = acc + sconv_ref[k] * cw_ref[k:k + 1, :]
    xc_ref[...] = _silu(acc)


def _mix_in_sample(x2d, mod_s, lw, spool_t, sconv_t):
    n = x2d.shape[0]
    full = lambda r, c: pl.BlockSpec((r, c), lambda i: (0, 0), pipeline_mode=pl.Buffered(1))
    full3 = lambda a, r, c: pl.BlockSpec((a, r, c), lambda i: (0, 0, 0), pipeline_mode=pl.Buffered(1))
    modspec = lambda k: pl.BlockSpec((n, D_MODEL), lambda i, k=k: (0, k),
                                     pipeline_mode=pl.Buffered(1))
    return pl.pallas_call(
        _mix_in_sample_kernel,
        grid=(1,),
        in_specs=[
            full(n, D_MODEL), modspec(0), modspec(1),
            _resident((1, D_MODEL)),
            _resident((D_MODEL, W_ALL_COLS)),
            _resident((D_MODEL, LANES)), _resident((1, LANES)),
            _resident((len(POOL_WINDOWS), POOL_GW, POOL_GW)), _resident((1, D_MODEL)),
            _resident((CONV_W, CONV_DIM)), _resident((1, CONV_DIM)),
            full3(POOL_BUF, n, D_MODEL), full3(CONV_W - 1, n, CONV_DIM),
        ],
        out_specs=[
            full(n, D_MODEL), full(n, D_MODEL), full(n, D_INNER), full(n, CONV_DIM), full(n, LANES),
            full3(POOL_BUF, n, D_MODEL), full3(CONV_W - 1, n, CONV_DIM),
        ],
        out_shape=[
            jax.ShapeDtypeStruct((n, D_MODEL), F32),
            jax.ShapeDtypeStruct((n, D_MODEL), F32),
            jax.ShapeDtypeStruct((n, D_INNER), F32),
            jax.ShapeDtypeStruct((n, CONV_DIM), F32),
            jax.ShapeDtypeStruct((n, LANES), F32),
            jax.ShapeDtypeStruct((POOL_BUF, n, D_MODEL), F32),
            jax.ShapeDtypeStruct((CONV_W - 1, n, CONV_DIM), F32),
        ],
        compiler_params=pltpu.CompilerParams(
            dimension_semantics=("arbitrary",), vmem_limit_bytes=VMEM_LIMIT),
        name="mix_in_sample",
    )(x2d, mod_s, mod_s, lw["n1w"], lw["w_all"], lw["wdt"], lw["dtb"],
      lw["wpool"], lw["pscale"], lw["cw"], lw["cb"], spool_t, sconv_t)


def _ssm_step_kernel(h_ref, xc_ref, dt_ref, alog_ref, dexp_ref, hout_ref, y_ref, *, bb):
    lane = lax.broadcasted_iota(jnp.int32, (bb, LANES), 1)
    head_ok = lane < HEADS
    dt = jnp.where(head_ok, dt_ref[...], 0.0)
    a_neg = jnp.where(head_ok[0:1, :], -jnp.exp(alog_ref[...]), 0.0)
    dec = jnp.exp(dt * a_neg)
    xs = xc_ref[:, 0:D_INNER]
    dt_exp = jnp.concatenate([_group_expand(dt, g) for g in range(GROUPS)], axis=1)
    pad = jnp.zeros((bb, D_INNER), F32)
    xdt = jnp.concatenate([xs * dt_exp, pad], axis=0).astype(BF16)
    bm = xc_ref[:, D_INNER:D_INNER + GROUPS * D_STATE]
    cm = xc_ref[:, D_INNER + GROUPS * D_STATE:]
    rowid = lax.broadcasted_iota(jnp.int32, (2 * bb, D_STATE), 0)
    rowid_y = lax.broadcasted_iota(jnp.int32, (2 * bb, GROUP_W), 0)
    padn = jnp.zeros((bb, D_STATE), F32)

    y_acc = [jnp.zeros((2 * bb, GROUP_W), F32) for _ in range(GROUPS)]
    for j in range(bb):
        for g in range(GROUPS):
            rs = slice(g * GROUP_W, (g + 1) * GROUP_W)
            ns = slice(g * D_STATE, (g + 1) * D_STATE)
            h_g = h_ref[j, rs, :]
            dcol = jnp.concatenate(
                [jnp.broadcast_to(dec[j:j + 1, HPG * g + r:HPG * g + r + 1], (HEADDIM, D_STATE))
                 for r in range(HPG)], axis=0)
            b16 = jnp.concatenate([bm[:, ns], padn], axis=0)
            b_j = jnp.where(rowid == j, b16, 0.0).astype(BF16)
            new = h_g * dcol + _dot_tn(xdt[:, rs], b_j)
            hout_ref[j, rs, :] = new
            c16 = jnp.concatenate([cm[:, ns], padn], axis=0).astype(BF16)
            y_all = _dot_nt(c16, new.astype(BF16))
            y_acc[g] = y_acc[g] + jnp.where(rowid_y == j, y_all, 0.0)
    y = jnp.concatenate(y_acc, axis=1)[0:bb, :]
    y_ref[...] = y + dexp_ref[...] * xs


def _ssm_step(h3, xc, dt, lw, bb=8):
    n = h3.shape[0]
    return pl.pallas_call(
        functools.partial(_ssm_step_kernel, bb=bb),
        grid=(n // bb,),
        in_specs=[
            pl.BlockSpec((bb, HEADS * HEADDIM, D_STATE), lambda i: (i, 0, 0)),
            pl.BlockSpec((bb, CONV_DIM), lambda i: (i, 0)),
            pl.BlockSpec((bb, LANES), lambda i: (i, 0)),
            _resident((1, LANES)), _resident((1, D_INNER)),
        ],
        out_specs=[
            pl.BlockSpec((bb, HEADS * HEADDIM, D_STATE), lambda i: (i, 0, 0)),
            pl.BlockSpec((bb, D_INNER), lambda i: (i, 0)),
        ],
        out_shape=[
            jax.ShapeDtypeStruct(h3.shape, F32),
            jax.ShapeDtypeStruct((n, D_INNER), F32),
        ],
        compiler_params=pltpu.CompilerParams(
            dimension_semantics=("arbitrary",), vmem_limit_bytes=VMEM_LIMIT),
        name="ssm_step",
    )(h3, xc, dt, lw["alog"], lw["dexp"])


def _out_ffn_sample_kernel(y_ref, sz_ref, ag_ref, gb_ref, x_ref, g1_ref, sh_ref, sc_ref, g2_ref,
                           nw_ref, wproj_ref, wout_ref, n2w_ref, win_ref, wffn_out_ref, fnw_ref,
                           o_ref, act_scr, *, final):
    x1 = _mixer_tail(y_ref[...], sz_ref[...], ag_ref[...], gb_ref[...], x_ref[...],
                     g1_ref[...], nw_ref[...], wproj_ref, wout_ref)
    o_ref[...] = _ffn_rows(x1, sh_ref[...], sc_ref[...], g2_ref[...], n2w_ref[...],
                           win_ref, wffn_out_ref, fnw_ref[...], act_scr, final)


def _out_ffn_sample(y, sz, ag, gb, x2d, mod_s, lw, fnw, final):
    n = x2d.shape[0]
    full = lambda c: pl.BlockSpec((n, c), lambda i: (0, 0))
    modspec = lambda k: pl.BlockSpec((n, D_MODEL), lambda i, k=k: (0, k))
    return pl.pallas_call(
        functools.partial(_out_ffn_sample_kernel, final=final),
        grid=(1,),
        in_specs=[
            full(D_INNER), full(D_INNER), full(D_MODEL), full(D_MODEL), full(D_MODEL),
            modspec(2), modspec(3), modspec(4), modspec(5),
            _resident((1, D_INNER)), _weight((D_INNER, D_MODEL)), _weight((D_MODEL, D_MODEL)),
            _resident((1, D_MODEL)),
            _resident((D_MODEL, 2 * D_FF)), _weight((D_FF, D_MODEL)),
            _resident((1, D_MODEL)),
        ],
        out_specs=full(D_MODEL),
        out_shape=jax.ShapeDtypeStruct((n, D_MODEL), F32),
        scratch_shapes=[pltpu.VMEM((n, D_FF), BF16)],
        compiler_params=pltpu.CompilerParams(
            dimension_semantics=("arbitrary",), vmem_limit_bytes=VMEM_LIMIT),
        name="out_ffn_sample",
    )(y, sz, ag, gb, x2d, mod_s, mod_s, mod_s, mod_s, lw["ssd_nw"], lw["wproj"], lw["wout"],
      lw["n2w"], lw["wffn_in"], lw["wffn_out"], fnw)


def _layer_weights(l, w_in, w_pool, pool_scale, conv_w, conv_b, dt_bias, A_log, D_skip, ssd_norm_w,
                   w_ssd_proj, w_out, norm1_w, norm2_w, w_ffn_in, w_ffn_out):
    rep = LANES // HEADS
    lane_id = jnp.arange(LANES)[:, None]
    col_head = jnp.arange(D_INNER)[None, :] // HEADDIM
    hexp = ((lane_id % HEADS == col_head) & (lane_id < 3 * HEADS)).astype(BF16)
    return dict(
        n1w=norm1_w[l].reshape(1, D_MODEL),
        **_prep_w_in(w_in, l),
        dtb=jnp.tile(dt_bias[l], rep).reshape(1, LANES),
        wpool=w_pool[l].astype(BF16),
        pscale=pool_scale[l].reshape(1, D_MODEL),
        cw=conv_w[l],
        cb=conv_b[l].reshape(1, CONV_DIM),
        alog=jnp.tile(A_log[l], rep).reshape(1, LANES),
        hexp=hexp,
        dexp=jnp.repeat(D_skip[l], HEADDIM).reshape(1, D_INNER),
        ssd_nw=ssd_norm_w[l].reshape(1, D_INNER),
        wproj=_cast_weight(w_ssd_proj, l),
        wout=_cast_weight(w_out, l),
        n2w=norm2_w[l].reshape(1, D_MODEL),
        wffn_in=_cast_weight(w_ffn_in, l, pad=0),
        wffn_out=_cast_weight(w_ffn_out, l),
    )


def kernel(x_prompt, x_sample, c_prompt, c_sample, state_pool, state_conv, state_ssm, w_ada, b_ada, norm1_w, w_in, w_pool, pool_scale, conv_w, conv_b, dt_bias, A_log, D_skip, ssd_norm_w, w_ssd_proj, w_out, norm2_w, w_ffn_in, w_ffn_out, final_norm_w):
    depth = w_in.shape[0]
    bsz, L, _ = x_prompt.shape
    nsmp = x_sample.shape[0]
    xp = x_prompt
    xs = x_sample.reshape(nsmp, D_MODEL)
    c_all = jnp.concatenate([c_prompt, c_sample], axis=0)
    fnw = final_norm_w.reshape(1, D_MODEL)
    pool_p, conv_p, ssm_p, pool_s, conv_s, ssm_s = [], [], [], [], [], []
    for l in range(depth):
        final = l == depth - 1
        lw = _layer_weights(l, w_in, w_pool, pool_scale, conv_w, conv_b, dt_bias, A_log, D_skip,
                            ssd_norm_w, w_ssd_proj, w_out, norm1_w, norm2_w, w_ffn_in, w_ffn_out)
        mod = _modulation(c_all, w_ada[l], b_ada[l])
        mod_p = mod[:bsz].reshape(bsz, 1, N_MOD * D_MODEL)
        mod_s = mod[bsz:]

        ag, gb, sz, xc, dt, pst, cst = _mix_in_prompt(xp, mod_p, lw)
        x1, hst = _ssd_prompt(xc, dt, sz, ag, gb, xp, mod_p, lw)
        tm = 1024
        per = L // tm
        p_mods = (mod_p, lambda k: pl.BlockSpec((None, 1, D_MODEL), lambda i, k=k: (i // per, 0, k)))
        xp = _ffn(x1.reshape(bsz * L, D_MODEL), p_mods, lw, fnw, final, tm).reshape(bsz, L, D_MODEL)
        pool_p.append(pst)
        conv_p.append(cst)
        ssm_p.append(hst.reshape(bsz, HEADS, HEADDIM, D_STATE))

        ag, gb, sz, xc, dt, pnew_t, cnew_t = _mix_in_sample(
            xs, mod_s, lw, jnp.swapaxes(state_pool[l], 0, 1), jnp.swapaxes(state_conv[l], 0, 1))
        hnew, y = _ssm_step(state_ssm[l].reshape(nsmp, HEADS * HEADDIM, D_STATE), xc, dt, lw)
        xs = _out_ffn_sample(y, sz, ag, gb, xs, mod_s, lw, fnw, final)
        pool_s.append(jnp.swapaxes(pnew_t, 0, 1))
        conv_s.append(jnp.swapaxes(cnew_t, 0, 1))
        ssm_s.append(hnew.reshape(nsmp, HEADS, HEADDIM, D_STATE))

    return (xp, xs.reshape(nsmp, 1, D_MODEL), jnp.stack(pool_p), jnp.stack(conv_p), jnp.stack(ssm_p),
            jnp.stack(pool_s), jnp.stack(conv_s), jnp.stack(ssm_s))
```
